```python
import numpy as np
import jax, jax.numpy as jnp
from jax import lax

D_MODEL = 1024
BATCH = 8
SEQ = 2048
DEPTH = 2

GRID_W = 64
CTX_LEN = 256
N_MIXERS = 2
N_HEADS = 16
HEAD_DIM = D_MODEL // N_HEADS
NA_KH = 8
NA_KW = 16
NA_QCB = NA_KW
NA_KCB = 2 * NA_KW
WA_KV_HEADS = 4
WA_GROUPS = N_HEADS // WA_KV_HEADS
WA_WINDOW = 128
WA_BLOCK = 128
D_FF = 2816
FFN_RES = 0.5
ROPE_BASE = 10000.0
N_MOD = 9
ALPHA = (2 * DEPTH) ** 0.25
BETA = (8 * DEPTH) ** -0.25
LN_EPS = 1e-5
NEG_INF = -1e30
N_NA_LAYERS = (DEPTH + 1) // 2
N_WA_LAYERS = DEPTH // 2

kernel_name = "hybrid_natten_swa_macaron_deepnorm"


def layer_norm(x, g, b):
    xf = x.astype(jnp.float32)
    mu = jnp.mean(xf, axis=-1, keepdims=True)
    var = jnp.mean(jnp.square(xf - mu), axis=-1, keepdims=True)
    return ((xf - mu) * lax.rsqrt(var + LN_EPS)).astype(x.dtype) * g + b


def modulate(h, shift, scale):
    return h * (1.0 + scale) + shift


def swiglu(u, w_in, w_out):
    a, v = jnp.split(u @ w_in, 2, axis=-1)
    return (jax.nn.silu(a) * v) @ w_out


def ffn_sublayer(h, shift, scale, gate, w_in, w_out, g, b):
    y = swiglu(modulate(h, shift, scale), w_in, w_out)
    return layer_norm(ALPHA * h + FFN_RES * gate * y, g, b)


def axial_rope(x):
    S = x.shape[1]
    t = jnp.arange(S, dtype=jnp.int32)
    rows = (t // GRID_W).astype(jnp.float32)
    cols = (t % GRID_W).astype(jnp.float32)
    n = HEAD_DIM // 4
    inv_freq = ROPE_BASE ** (-jnp.arange(n, dtype=jnp.float32) / n)
    half = HEAD_DIM // 2

    def rot(xa, pos):
        ang = pos[:, None] * inv_freq[None, :]
        cos = jnp.cos(ang)[None, :, None, :].astype(x.dtype)
        sin = jnp.sin(ang)[None, :, None, :].astype(x.dtype)
        x1, x2 = xa[..., :n], xa[..., n:]
        return jnp.concatenate([x1 * cos - x2 * sin, x2 * cos + x1 * sin], axis=-1)

    return jnp.concatenate([rot(x[..., :half], rows), rot(x[..., half:], cols)], axis=-1)


def ctx_attention(qc, kc, vc, sink):
    B, C, H, d = qc.shape
    hk = kc.shape[2]
    g = H // hk
    qg = qc.reshape(B, C, hk, g, d)
    s = jnp.einsum('bqhgd,bkhd->bhgqk', qg, kc).astype(jnp.float32)
    if sink is not None:
        sk = jnp.broadcast_to(sink.astype(jnp.float32).reshape(1, hk, g, 1, 1), s.shape[:-1] + (1,))
        s = jnp.concatenate([s, sk], axis=-1)
    p = jax.nn.softmax(s, axis=-1)[..., :C].astype(vc.dtype)
    o = jnp.einsum('bhgqk,bkhd->bqhgd', p, vc)
    return o.reshape(B, C, H * d)


def neighborhood_attention(u_lat, u_ctx, w_qkv, w_o, rpb, with_ctx_out):
    B, S, _ = u_lat.shape
    C = u_ctx.shape[1]
    rows = S // GRID_W
    kh = min(NA_KH, rows)
    scale = HEAD_DIM ** -0.5
    qkv = (u_lat @ w_qkv).reshape(B, rows, GRID_W, 3, N_HEADS, HEAD_DIM)
    q = qkv[:, :, :, 0] * scale
    k = qkv[:, :, :, 1]
    v = qkv[:, :, :, 2]
    kvc = (u_ctx @ w_qkv[:, D_MODEL:]).reshape(B, C, 2, N_HEADS, HEAD_DIM)
    kc, vc = kvc[:, :, 0], kvc[:, :, 1]

    ncb = GRID_W // NA_QCB
    qcol = np.arange(GRID_W).reshape(ncb, NA_QCB)
    blk_start = np.clip(np.arange(ncb) * NA_QCB - NA_KW // 2, 0, GRID_W - NA_KCB)
    kcol = blk_start[:, None] + np.arange(NA_KCB)[None, :]
    win_start = np.clip(qcol - NA_KW // 2, 0, GRID_W - NA_KW)
    col_valid = (kcol[:, None, :] >= win_start[..., None]) & (kcol[:, None, :] < win_start[..., None] + NA_KW)
    col_off = np.clip(kcol[:, None, :] - qcol[..., None], -(NA_KW - 1), NA_KW - 1) + NA_KW - 1
    bias_cols = jnp.where(col_valid, rpb[:, :, col_off].astype(jnp.float32), NEG_INF)
    n_win = kh * NA_KCB

    def row_block(r):
        rs = jnp.clip(r - kh // 2, 0, rows - kh)
        q_r = lax.dynamic_index_in_dim(q, r, axis=1, keepdims=False).reshape(B, ncb, NA_QCB, N_HEADS, HEAD_DIM)
        k_r = lax.dynamic_slice_in_dim(k, rs, kh, axis=1)[:, :, kcol]
        v_r = lax.dynamic_slice_in_dim(v, rs, kh, axis=1)[:, :, kcol]
        row_idx = rs + jnp.arange(kh) - r + NA_KH - 1
        bias = jnp.take(bias_cols, row_idx, axis=1).transpose(0, 2, 3, 1, 4)[None]
        s_win = jnp.einsum('bcqhd,bicjhd->bhcqij', q_r, k_r).astype(jnp.float32) + bias
        s_win = s_win.reshape(B, N_HEADS, ncb, NA_QCB, n_win)
        s_ctx = jnp.einsum('bcqhd,bkhd->bhcqk', q_r, kc).astype(jnp.float32)
        p = jax.nn.softmax(jnp.concatenate([s_win, s_ctx], axis=-1), axis=-1).astype(v.dtype)
        p_win = p[..., :n_win].reshape(B, N_HEADS, ncb, NA_QCB, kh, NA_KCB)
        p_ctx = p[..., n_win:]
        o = (jnp.einsum('bhcqij,bicjhd->bcqhd', p_win, v_r)
             + jnp.einsum('bhcqk,bkhd->bcqhd', p_ctx, vc))
        return o.reshape(B, GRID_W, D_MODEL)

    out = lax.map(row_block, jnp.arange(rows, dtype=jnp.int32))
    y_lat = out.transpose(1, 0, 2, 3).reshape(B, S, D_MODEL) @ w_o
    y_ctx = None
    if with_ctx_out:
        qc = (u_ctx @ w_qkv[:, :D_MODEL]).reshape(B, C, N_HEADS, HEAD_DIM) * scale
        y_ctx = ctx_attention(qc, kc, vc, None) @ w_o
    return y_lat, y_ctx


def window_gqa(u_lat, u_ctx, w_qkv, w_o, sinks, with_ctx_out):
    B, S, _ = u_lat.shape
    C = u_ctx.shape[1]
    dq = N_HEADS * HEAD_DIM
    dkv = WA_KV_HEADS * HEAD_DIM
    scale = HEAD_DIM ** -0.5
    qkv = u_lat @ w_qkv
    q = axial_rope(qkv[..., :dq].reshape(B, S, N_HEADS, HEAD_DIM)) * scale
    k = axial_rope(qkv[..., dq:dq + dkv].reshape(B, S, WA_KV_HEADS, HEAD_DIM))
    v = qkv[..., dq + dkv:].reshape(B, S, WA_KV_HEADS, HEAD_DIM)
    kvc = u_ctx @ w_qkv[:, dq:]
    kc = kvc[..., :dkv].reshape(B, C, WA_KV_HEADS, HEAD_DIM)
    vc = kvc[..., dkv:].reshape(B, C, WA_KV_HEADS, HEAD_DIM)

    nb = S // WA_BLOCK
    pad = ((0, 0), (WA_BLOCK, WA_BLOCK), (0, 0), (0, 0))
    k_pad = jnp.pad(k, pad)
    v_pad = jnp.pad(v, pad)
    q_blk = q.reshape(B, nb, WA_BLOCK, WA_KV_HEADS, WA_GROUPS, HEAD_DIM)
    sink = sinks.astype(jnp.float32).reshape(1, WA_KV_HEADS, WA_GROUPS, 1, 1)
    qi = jnp.arange(WA_BLOCK, dtype=jnp.int32)
    kj = jnp.arange(3 * WA_BLOCK, dtype=jnp.int32)
    n_win = 3 * WA_BLOCK

    def band_block(b):
        qb = lax.dynamic_index_in_dim(q_blk, b, axis=1, keepdims=False)
        kb = lax.dynamic_slice_in_dim(k_pad, b * WA_BLOCK, n_win, axis=1)
        vb = lax.dynamic_slice_in_dim(v_pad, b * WA_BLOCK, n_win, axis=1)
        pos_q = b * WA_BLOCK + qi
        pos_k = (b - 1) * WA_BLOCK + kj
        valid = ((jnp.abs(pos_q[:, None] - pos_k[None, :]) <= WA_WINDOW)
                 & (pos_k >= 0)[None, :] & (pos_k < S)[None, :])
        s_win = jnp.where(valid, jnp.einsum('bqhgd,bkhd->bhgqk', qb, kb).astype(jnp.float32), NEG_INF)
        s_ctx = jnp.einsum('bqhgd,bkhd->bhgqk', qb, kc).astype(jnp.float32)
        s_sink = jnp.broadcast_to(sink, s_win.shape[:-1] + (1,))
        p = jax.nn.softmax(jnp.concatenate([s_win, s_ctx, s_sink], axis=-1), axis=-1).astype(v.dtype)
        o = (jnp.einsum('bhgqk,bkhd->bqhgd', p[..., :n_win], vb)
             + jnp.einsum('bhgqk,bkhd->bqhgd', p[..., n_win:n_win + C], vc))
        return o.reshape(B, WA_BLOCK, D_MODEL)

    out = lax.map(band_block, jnp.arange(nb, dtype=jnp.int32))
    y_lat = out.transpose(1, 0, 2, 3).reshape(B, S, D_MODEL) @ w_o
    y_ctx = None
    if with_ctx_out:
        qc = (u_ctx @ w_qkv[:, :dq]).reshape(B, C, N_HEADS, HEAD_DIM) * scale
        y_ctx = ctx_attention(qc, kc, vc, sinks) @ w_o
    return y_lat, y_ctx


def setup_inputs(seed: int = 0) -> dict:
    key = jax.random.key(seed)
    ks = jax.random.split(key, 16)
    f32 = jnp.float32
    dq = N_HEADS * HEAD_DIM
    dkv = WA_KV_HEADS * HEAD_DIM

    def nrm(k, shape, s):
        return jax.random.normal(k, shape, f32) * s

    return {
        "x": nrm(ks[0], (BATCH, SEQ, D_MODEL), 1.0),
        "c": nrm(ks[1], (BATCH, D_MODEL), 1.0),
        "ctx": nrm(ks[2], (BATCH, CTX_LEN, D_MODEL), 1.0),
        "c_ctx": nrm(ks[3], (D_MODEL,), 1.0),
        "w_mod": nrm(ks[4], (DEPTH, D_MODEL, N_MOD * D_MODEL), 0.5 * D_MODEL ** -0.5),
        "b_mod": nrm(ks[5], (DEPTH, N_MOD * D_MODEL), 0.02),
        "ln_g": 1.0 + nrm(ks[6], (DEPTH, 3, D_MODEL), 0.02),
        "ln_b": nrm(ks[7], (DEPTH, 3, D_MODEL), 0.02),
        "ffn_w_in": nrm(ks[8], (DEPTH, 2, D_MODEL, 2 * D_FF), D_MODEL ** -0.5),
        "ffn_w_out": nrm(ks[9], (DEPTH, 2, D_FF, D_MODEL), BETA * D_FF ** -0.5),
        "na_w_qkv": nrm(ks[10], (N_NA_LAYERS, D_MODEL, 3 * dq), D_MODEL ** -0.5),
        "na_w_o": nrm(ks[11], (N_NA_LAYERS, dq, D_MODEL), BETA * dq ** -0.5),
        "na_rpb": nrm(ks[12], (N_NA_LAYERS, N_HEADS, 2 * NA_KH - 1, 2 * NA_KW - 1), 0.1),
        "wa_w_qkv": nrm(ks[13], (N_WA_LAYERS, D_MODEL, dq + 2 * dkv), D_MODEL ** -0.5),
        "wa_w_o": nrm(ks[14], (N_WA_LAYERS, dq, D_MODEL), BETA * dq ** -0.5),
        "wa_sinks": nrm(ks[15], (N_WA_LAYERS, N_HEADS), 0.5),
    }


def reference(x, c, ctx, c_ctx, w_mod, b_mod, ln_g, ln_b, ffn_w_in, ffn_w_out,
              na_w_qkv, na_w_o, na_rpb, wa_w_qkv, wa_w_o, wa_sinks):
    h_lat, h_ctx = x, ctx
    silu_c = jax.nn.silu(c)
    silu_cc = jax.nn.silu(c_ctx)
    for i in range(DEPTH):
        ctx_needed = i < DEPTH - 1
        m_lat = jnp.split((silu_c @ w_mod[i] + b_mod[i])[:, None, :], N_MOD, axis=-1)
        m_ctx = jnp.split((silu_cc @ w_mod[i] + b_mod[i])[None, None, :], N_MOD, axis=-1)

        h_lat = ffn_sublayer(h_lat, m_lat[0], m_lat[1], m_lat[2], ffn_w_in[i, 0], ffn_w_out[i, 0], ln_g[i, 0], ln_b[i, 0])
        h_ctx = ffn_sublayer(h_ctx, m_ctx[0], m_ctx[1], m_ctx[2], ffn_w_in[i, 0], ffn_w_out[i, 0], ln_g[i, 0], ln_b[i, 0])

        u_lat = modulate(h_lat, m_lat[3], m_lat[4])
        u_ctx = modulate(h_ctx, m_ctx[3], m_ctx[4])
        j = i // N_MIXERS
        if i % N_MIXERS == 0:
            y_lat, y_ctx = neighborhood_attention(u_lat, u_ctx, na_w_qkv[j], na_w_o[j], na_rpb[j], ctx_needed)
        else:
            y_lat, y_ctx = window_gqa(u_lat, u_ctx, wa_w_qkv[j], wa_w_o[j], wa_sinks[j], ctx_needed)
        h_lat = layer_norm(ALPHA * h_lat + m_lat[5] * y_lat, ln_g[i, 1], ln_b[i, 1])

        h_lat = ffn_sublayer(h_lat, m_lat[6], m_lat[7], m_lat[8], ffn_w_in[i, 1], ffn_w_out[i, 1], ln_g[i, 2], ln_b[i, 2])
        if ctx_needed:
            h_ctx = layer_norm(ALPHA * h_ctx + m_ctx[5] * y_ctx, ln_g[i, 1], ln_b[i, 1])
            h_ctx = ffn_sublayer(h_ctx, m_ctx[6], m_ctx[7], m_ctx[8], ffn_w_in[i, 1], ffn_w_out[i, 1], ln_g[i, 2], ln_b[i, 2])
    return h_lat
```

```python
import functools

import numpy as np
import jax
import jax.numpy as jnp
from jax import lax
from jax.experimental import pallas as pl
from jax.experimental.pallas import tpu as pltpu

F32 = jnp.float32
BF16 = jnp.bfloat16

GRID_W = 64
N_HEADS = 16
HEAD_DIM = 64
NA_KH = 8
NA_KW = 16
WA_KV_HEADS = 4
WA_WINDOW = 128
FFN_RES = 0.5
ROPE_BASE = 10000.0
N_MOD = 9
LN_EPS = 1e-5
NEG_INF = -1e30

LANES = 128
TOKEN_TILE = 512
NA_ROWS_PER_STEP = 8
VMEM_LIMIT = 56 * 1024 * 1024


def _cparams(n_grid):
    return pltpu.CompilerParams(
        dimension_semantics=("arbitrary",) * n_grid, vmem_limit_bytes=VMEM_LIMIT)


def _resident(shape):
    nd = len(shape)
    return pl.BlockSpec(shape, lambda *_: (0,) * nd, pipeline_mode=pl.Buffered(1))


def _layer_norm(z, g, b):
    mu = jnp.mean(z, axis=-1, keepdims=True)
    zc = z - mu
    var = jnp.mean(zc * zc, axis=-1, keepdims=True)
    return zc * lax.rsqrt(var + LN_EPS) * g + b


def _swiglu(u_bf16, win_ref, wout_ref):
    d_ff = wout_ref.shape[0]
    a = jnp.dot(u_bf16, win_ref[:, :d_ff], preferred_element_type=F32)
    v = jnp.dot(u_bf16, win_ref[:, d_ff:], preferred_element_type=F32)
    g = (a * jax.nn.sigmoid(a) * v).astype(BF16)
    return jnp.dot(g, wout_ref[...], preferred_element_type=F32)


def _ffn_sublayer(x, mod, base, win_ref, wout_ref, g, b, alpha):
    shift, scale, gate = mod[base:base + 1], mod[base + 1:base + 2], mod[base + 2:base + 3]
    u = (x * (1.0 + scale) + shift).astype(BF16)
    y = _swiglu(u, win_ref, wout_ref)
    return _layer_norm(alpha * x + (FFN_RES * gate) * y, g, b)


def _mod_body(c_ref, w_ref, b_ref, o_ref):
    c = c_ref[...]
    s = (c * jax.nn.sigmoid(c)).astype(BF16)
    o_ref[0] = jnp.dot(s, w_ref[0].astype(BF16), preferred_element_type=F32) + b_ref[0]


def _modulation(cc, w_mod, b_mod):
    depth, d, nd = w_mod.shape
    r = cc.shape[0]
    out = pl.pallas_call(
        _mod_body,
        grid=(depth, nd // d),
        in_specs=[
            pl.BlockSpec((r, d), lambda i, j: (0, 0)),
            pl.BlockSpec((1, d, d), lambda i, j: (i, 0, j)),
            pl.BlockSpec((1, 1, d), lambda i, j: (i, 0, j)),
        ],
        out_specs=pl.BlockSpec((1, r, d), lambda i, j: (i, 0, j)),
        out_shape=jax.ShapeDtypeStruct((depth, r, nd), F32),
        compiler_params=_cparams(2),
        name="modulation",
    )(cc, w_mod, b_mod.reshape(depth, 1, nd))
    return out.reshape(depth, r, nd // d, d)


def _ffn_body(mod_ref, x_ref, win_ref, wout_ref, g_ref, b_ref, o_ref, *, alpha):
    o_ref[...] = _ffn_sublayer(x_ref[...], mod_ref[0, 0], 0, win_ref, wout_ref,
                               g_ref[0:1], b_ref[0:1], alpha)


def _post_body(mod_ref, h_ref, a_ref, wo_ref, win_ref, wout_ref, g_ref, b_ref, o_ref, *, alpha):
    mod = mod_ref[0, 0]
    y = jnp.dot(a_ref[...], wo_ref[...], preferred_element_type=F32)
    h = _layer_norm(alpha * h_ref[...] + mod[5:6] * y, g_ref[1:2], b_ref[1:2])
    o_ref[...] = _ffn_sublayer(h, mod, 6, win_ref, wout_ref, g_ref[2:3], b_ref[2:3], alpha)


def _qkv_na_body(mod_ref, h_ref, w_ref, q_ref, k_ref, v_ref):
    mod = mod_ref[0, 0]
    d = h_ref.shape[1]
    u = (h_ref[...] * (1.0 + mod[4:5]) + mod[3:4]).astype(BF16)
    r = jnp.dot(u, w_ref[...], preferred_element_type=F32)
    q_ref[...] = (r[:, :d] * HEAD_DIM ** -0.5).astype(BF16)
    k_ref[...] = r[:, d:2 * d].astype(BF16)
    v_ref[...] = r[:, 2 * d:].astype(BF16)


def _rope_lanes(x, cos, sin_signed, first_half):
    partner = jnp.where(first_half, pltpu.roll(x, LANES - 16, 1), pltpu.roll(x, 16, 1))
    return x * cos + partner * sin_signed


def _qkv_wa_body(mod_ref, h_ref, cos_ref, sin_ref, w_ref, q_ref, k_ref, v_ref):
    mod = mod_ref[0, 0]
    d = h_ref.shape[1]
    nk = k_ref.shape[1]
    u = (h_ref[...] * (1.0 + mod[4:5]) + mod[3:4]).astype(BF16)
    r = jnp.dot(u, w_ref[...], preferred_element_type=F32)
    cos, sin = cos_ref[...], sin_ref[...]
    first_half = lax.broadcasted_iota(jnp.int32, cos.shape, 1) % 32 < 16
    for j in range(d // LANES):
        sl = slice(j * LANES, (j + 1) * LANES)
        q_ref[:, sl] = (_rope_lanes(r[:, sl], cos, sin, first_half) * HEAD_DIM ** -0.5).astype(BF16)
    for j in range(nk // LANES):
        sl = slice(d + j * LANES, d + (j + 1) * LANES)
        k_ref[:, j * LANES:(j + 1) * LANES] = _rope_lanes(r[:, sl], cos, sin, first_half).astype(BF16)
    v_ref[...] = r[:, d + nk:].astype(BF16)


def _mod_spec(n_lat_tiles, tiles_per_seq, ctx_row, d):
    def idx(t):
        return (jnp.where(t < n_lat_tiles, t // tiles_per_seq, ctx_row), 0, 0)
    return pl.BlockSpec((1, 1, N_MOD, d), lambda t: (0,) + idx(t))


def _tok_spec(width):
    return pl.BlockSpec((TOKEN_TILE, width), lambda t: (t, 0))


def _split_heads(x):
    lo = lax.broadcasted_iota(jnp.int32, x.shape, 1) < HEAD_DIM
    zero = jnp.zeros_like(x)
    return jnp.concatenate([jnp.where(lo, x, zero), jnp.where(lo, zero, x)], axis=0)


def _merge_heads(o):
    n = o.shape[0] // 2
    lo = lax.broadcasted_iota(jnp.int32, (n, LANES), 1) < HEAD_DIM
    return jnp.where(lo, o[:n], o[n:])


def _qk(q, k):
    return lax.dot_general(q, k, (((1,), (1,)), ((), ())), preferred_element_type=F32)


def _na_body(q_ref, k_ref, v_ref, kc_ref, vc_ref, t2_ref, o_ref):
    rows = k_ref.shape[0] // GRID_W
    n_win = NA_KH * GRID_W
    rb = pl.program_id(1)

    def row_body(i, carry):
        r = rb * NA_ROWS_PER_STEP + i
        rs = jnp.clip(r - NA_KH // 2, 0, rows - NA_KH)
        shift = rs - r + NA_KH - 1
        q_off = pl.multiple_of(i * GRID_W, GRID_W)
        k_off = pl.multiple_of(rs * GRID_W, GRID_W)
        for p in range(N_HEADS // 2):
            sl = slice(p * LANES, (p + 1) * LANES)
            qs = _split_heads(q_ref[pl.ds(q_off, GRID_W), sl])
            s_win = _qk(qs, k_ref[pl.ds(k_off, n_win), sl])
            bias = jnp.concatenate(
                [jnp.concatenate([t2_ref[2 * p, shift + 2 * jj], t2_ref[2 * p + 1, shift + 2 * jj]], axis=0)
                 for jj in range(NA_KH // 2)], axis=1)
            s_win = s_win + bias
            s_ctx = _qk(qs, kc_ref[:, sl])
            m = jnp.maximum(jnp.max(s_win, axis=-1, keepdims=True), jnp.max(s_ctx, axis=-1, keepdims=True))
            p_win = jnp.exp(s_win - m)
            p_ctx = jnp.exp(s_ctx - m)
            l = jnp.sum(p_win, axis=-1, keepdims=True) + jnp.sum(p_ctx, axis=-1, keepdims=True)
            o = (jnp.dot(p_win.astype(BF16), v_ref[pl.ds(k_off, n_win), sl], preferred_element_type=F32)
                 + jnp.dot(p_ctx.astype(BF16), vc_ref[:, sl], preferred_element_type=F32))
            o_ref[pl.ds(q_off, GRID_W), sl] = _merge_heads(o / l).astype(BF16)
        return carry

    lax.fori_loop(0, NA_ROWS_PER_STEP, row_body, 0)


def _ctx_body(q_ref, k_ref, v_ref, o_ref):
    for p in range(N_HEADS // 2):
        sl = slice(p * LANES, (p + 1) * LANES)
        qs = _split_heads(q_ref[:, sl])
        s = _qk(qs, k_ref[:, sl])
        m = jnp.max(s, axis=-1, keepdims=True)
        e = jnp.exp(s - m)
        l = jnp.sum(e, axis=-1, keepdims=True)
        o = jnp.dot(e.astype(BF16), v_ref[:, sl], preferred_element_type=F32)
        o_ref[:, sl] = _merge_heads(o / l).astype(BF16)


def _wa_body(sink_ref, q_ref, k_ref, v_ref, kc_ref, vc_ref, o_ref):
    seq = k_ref.shape[0]
    blk = q_ref.shape[0]
    n_win = 3 * blk
    qb = pl.program_id(1)
    ws = pl.multiple_of(jnp.clip((qb - 1) * blk, 0, seq - n_win), blk)
    pos_q = qb * blk + lax.broadcasted_iota(jnp.int32, (blk, n_win), 0)
    pos_k = ws + lax.broadcasted_iota(jnp.int32, (blk, n_win), 1)
    band = jnp.where(jnp.abs(pos_q - pos_k) <= WA_WINDOW, 0.0, NEG_INF).astype(F32)
    band = jnp.concatenate([band] * (N_HEADS // WA_KV_HEADS), axis=0)
    groups = N_HEADS // WA_KV_HEADS
    for g in range(WA_KV_HEADS):
        sl = slice(g * LANES, (g + 1) * LANES)
        qs = jnp.concatenate(
            [_split_heads(q_ref[:, (2 * g + j) * LANES:(2 * g + j + 1) * LANES]) for j in range(groups // 2)], axis=0)
        sink = jnp.concatenate(
            [jnp.full((blk, 1), sink_ref[g * groups + j], F32) for j in range(groups)], axis=0)
        s_win = _qk(qs, k_ref[pl.ds(ws, n_win), sl]) + band
        s_ctx = _qk(qs, kc_ref[:, sl])
        m = jnp.maximum(jnp.maximum(jnp.max(s_win, axis=-1, keepdims=True),
                                    jnp.max(s_ctx, axis=-1, keepdims=True)), sink)
        p_win = jnp.exp(s_win - m)
        p_ctx = jnp.exp(s_ctx - m)
        l = (jnp.sum(p_win, axis=-1, keepdims=True) + jnp.sum(p_ctx, axis=-1, keepdims=True)
             + jnp.exp(sink - m))
        o = (jnp.dot(p_win.astype(BF16), v_ref[pl.ds(ws, n_win), sl], preferred_element_type=F32)
             + jnp.dot(p_ctx.astype(BF16), vc_ref[:, sl], preferred_element_type=F32)) / l
        for j in range(groups // 2):
            o_ref[:, (2 * g + j) * LANES:(2 * g + j + 1) * LANES] = _merge_heads(
                o[2 * j * blk:2 * (j + 1) * blk]).astype(BF16)


def _na_bias_table(rpb):
    qcol = np.arange(GRID_W)[:, None]
    kcol = np.arange(GRID_W)[None, :]
    win_start = np.clip(qcol - NA_KW // 2, 0, GRID_W - NA_KW)
    valid = (kcol >= win_start) & (kcol < win_start + NA_KW)
    off = np.clip(kcol - qcol, -(NA_KW - 1), NA_KW - 1) + NA_KW - 1
    t = jnp.where(valid, rpb[:, :, off].astype(F32), NEG_INF)
    return jnp.concatenate([t[:, :-1], t[:, 1:]], axis=-1)


def _rope_tables(seq, pad_rows):
    t = jnp.arange(seq, dtype=jnp.int32)
    n = HEAD_DIM // 4
    inv_freq = ROPE_BASE ** (-jnp.arange(n, dtype=F32) / n)
    ang_r = (t // GRID_W).astype(F32)[:, None] * inv_freq[None, :]
    ang_c = (t % GRID_W).astype(F32)[:, None] * inv_freq[None, :]
    ang = jnp.concatenate([ang_r, ang_r, ang_c, ang_c], axis=1)
    sign = np.tile(np.concatenate([-np.ones(n), np.ones(n)]), 2).astype(np.float32)
    cos = jnp.cos(ang)
    sin = jnp.sin(ang) * sign
    cos = jnp.concatenate([cos, jnp.ones((pad_rows, HEAD_DIM), F32)], axis=0)
    sin = jnp.concatenate([sin, jnp.zeros((pad_rows, HEAD_DIM), F32)], axis=0)
    return jnp.tile(cos, (1, LANES // HEAD_DIM)), jnp.tile(sin, (1, LANES // HEAD_DIM))


def _dup_heads(w, n_heads):
    d = w.shape[0]
    w = w.reshape(d, n_heads, 1, HEAD_DIM)
    return jnp.broadcast_to(w, (d, n_heads, LANES // HEAD_DIM, HEAD_DIM)).reshape(d, n_heads * LANES)


def kernel(x, c, ctx, c_ctx, w_mod, b_mod, ln_g, ln_b, ffn_w_in, ffn_w_out,
           na_w_qkv, na_w_o, na_rpb, wa_w_qkv, wa_w_o, wa_sinks):
    bsz, seq, d = x.shape
    n_ctx = ctx.shape[1]
    depth = w_mod.shape[0]
    d_ff = ffn_w_out.shape[2]
    alpha = (2 * depth) ** 0.25
    t_lat, t_ctx = bsz * seq, bsz * n_ctx
    t_all = t_lat + t_ctx
    nt_lat, nt_all = t_lat // TOKEN_TILE, t_all // TOKEN_TILE
    tiles_per_seq = seq // TOKEN_TILE
    dq = N_HEADS * HEAD_DIM
    dkv = WA_KV_HEADS * HEAD_DIM

    mod_rows = 16
    cc = jnp.concatenate([c, c_ctx[None], jnp.zeros((mod_rows - bsz - 1, d), F32)], axis=0)
    mods = _modulation(cc, w_mod, b_mod)

    w_in = ffn_w_in.astype(BF16)
    w_out = ffn_w_out.astype(BF16)
    h = jnp.concatenate([x.reshape(t_lat, d), ctx.reshape(t_ctx, d)], axis=0)

    mod_spec = _mod_spec(nt_lat, tiles_per_seq, bsz, d)
    ln_spec = pl.BlockSpec((3, d), lambda t: (0, 0))

    for i in range(depth):
        last = i == depth - 1
        mods_i = mods[i:i + 1]
        is_na = i % 2 == 0
        j = i // 2

        h = pl.pallas_call(
            functools.partial(_ffn_body, alpha=alpha),
            grid=(nt_all,),
            in_specs=[mod_spec, _tok_spec(d), _resident((d, 2 * d_ff)), _resident((d_ff, d)), ln_spec, ln_spec],
            out_specs=_tok_spec(d),
            out_shape=jax.ShapeDtypeStruct((t_all, d), F32),
            compiler_params=_cparams(1),
            name=f"ffn_pre_{i}",
        )(mods_i, h, w_in[i, 0], w_out[i, 0], ln_g[i], ln_b[i])

        if is_na:
            w_qkv = na_w_qkv[j].astype(BF16)
            q, k, v = pl.pallas_call(
                _qkv_na_body,
                grid=(nt_all,),
                in_specs=[mod_spec, _tok_spec(d), _resident((d, 3 * dq))],
                out_specs=[_tok_spec(dq)] * 3,
                out_shape=[jax.ShapeDtypeStruct((t_all, dq), BF16)] * 3,
                compiler_params=_cparams(1),
                name=f"qkv_na_{i}",
            )(mods_i, h, w_qkv)
            t2 = _na_bias_table(na_rpb[j])
            q_rows = NA_ROWS_PER_STEP * GRID_W
            steps = seq // q_rows
            ctx_blk0 = t_lat // n_ctx
            att = pl.pallas_call(
                _na_body,
                grid=(bsz, steps),
                in_specs=[
                    pl.BlockSpec((q_rows, dq), lambda b, r: (b * steps + r, 0)),
                    pl.BlockSpec((seq, dq), lambda b, r: (b, 0)),
                    pl.BlockSpec((seq, dq), lambda b, r: (b, 0)),
                    pl.BlockSpec((n_ctx, dq), lambda b, r: (ctx_blk0 + b, 0)),
                    pl.BlockSpec((n_ctx, dq), lambda b, r: (ctx_blk0 + b, 0)),
                    _resident(t2.shape),
                ],
                out_specs=pl.BlockSpec((q_rows, dq), lambda b, r: (b * steps + r, 0)),
                out_shape=jax.ShapeDtypeStruct((t_lat, dq), BF16),
                compiler_params=_cparams(2),
                name=f"attn_na_{i}",
            )(q, k, v, k, v, t2)
            w_o = na_w_o[j].astype(BF16)
        else:
            wq = wa_w_qkv[j]
            w_qkv = jnp.concatenate(
                [wq[:, :dq], _dup_heads(wq[:, dq:dq + dkv], WA_KV_HEADS), _dup_heads(wq[:, dq + dkv:], WA_KV_HEADS)],
                axis=1).astype(BF16)
            nkv = WA_KV_HEADS * LANES
            cos, sin = _rope_tables(seq, TOKEN_TILE)
            rope_spec = pl.BlockSpec(
                (TOKEN_TILE, LANES), lambda t: (jnp.where(t < nt_lat, t % tiles_per_seq, tiles_per_seq), 0))
            q, k, v = pl.pallas_call(
                _qkv_wa_body,
                grid=(nt_all,),
                in_specs=[mod_spec, _tok_spec(d), rope_spec, rope_spec, _resident((d, dq + 2 * nkv))],
                out_specs=[_tok_spec(dq), _tok_spec(nkv), _tok_spec(nkv)],
                out_shape=[jax.ShapeDtypeStruct((t_all, dq), BF16),
                           jax.ShapeDtypeStruct((t_all, nkv), BF16),
                           jax.ShapeDtypeStruct((t_all, nkv), BF16)],
                compiler_params=_cparams(1),
                name=f"qkv_wa_{i}",
            )(mods_i, h, cos, sin, w_qkv)
            blk = WA_WINDOW
            steps = seq // blk
            ctx_blk0 = t_lat // n_ctx
            att = pl.pallas_call(
                _wa_body,
                grid=(bsz, steps),
                in_specs=[
                    pl.BlockSpec(memory_space=pltpu.SMEM),
                    pl.BlockSpec((blk, dq), lambda b, r: (b * steps + r, 0)),
                    pl.BlockSpec((seq, nkv), lambda b, r: (b, 0)),
                    pl.BlockSpec((seq, nkv), lambda b, r: (b, 0)),
                    pl.BlockSpec((n_ctx, nkv), lambda b, r: (ctx_blk0 + b, 0)),
                    pl.BlockSpec((n_ctx, nkv), lambda b, r: (ctx_blk0 + b, 0)),
                ],
                out_specs=pl.BlockSpec((blk, dq), lambda b, r: (b * steps + r, 0)),
                out_shape=jax.ShapeDtypeStruct((t_lat, dq), BF16),
                compiler_params=_cparams(2),
                name=f"attn_wa_{i}",
            )(wa_sinks[j], q, k, v, k, v)
            w_o = wa_w_o[j].astype(BF16)

        if not last:
            ctx_blk0 = t_lat // n_ctx
            ctx_spec = pl.BlockSpec((n_ctx, dq), lambda b: (ctx_blk0 + b, 0))
            att_ctx = pl.pallas_call(
                _ctx_body,
                grid=(bsz,),
                in_specs=[ctx_spec] * 3,
                out_specs=pl.BlockSpec((n_ctx, dq), lambda b: (b, 0)),
                out_shape=jax.ShapeDtypeStruct((t_ctx, dq), BF16),
                compiler_params=_cparams(1),
                name=f"attn_ctx_{i}",
            )(q, k, v)
            att = jnp.concatenate([att, att_ctx], axis=0)

        n_tiles = nt_lat if last else nt_all
        h = pl.pallas_call(
            functools.partial(_post_body, alpha=alpha),
            grid=(n_tiles,),
            in_specs=[mod_spec, _tok_spec(d), _tok_spec(dq), _resident((dq, d)),
                      _resident((d, 2 * d_ff)), _resident((d_ff, d)), ln_spec, ln_spec],
            out_specs=_tok_spec(d),
            out_shape=jax.ShapeDtypeStruct((n_tiles * TOKEN_TILE, d), F32),
            compiler_params=_cparams(1),
            name=f"post_{i}",
        )(mods_i, h, att, w_o, w_in[i, 1], w_out[i, 1], ln_g[i], ln_b[i])

    return h.reshape(bsz, seq, d)
```

```python
import functools

import numpy as np
import jax
import jax.numpy as jnp
from jax import lax
from jax.experimental import pallas as pl
from jax.experimental.pallas import tpu as pltpu

F32 = jnp.float32
BF16 = jnp.bfloat16

GRID_W = 64
N_HEADS = 16
HEAD_DIM = 64
NA_KH = 8
NA_KW = 16
WA_KV_HEADS = 4
WA_WINDOW = 128
FFN_RES = 0.5
ROPE_BASE = 10000.0
N_MOD = 9
LN_EPS = 1e-5
NEG_INF = -1e30

LANES = 128
TOKEN_TILE = 512
NA_ROWS_PER_STEP = 8
ATTN_LOOKAHEAD = 3
VMEM_LIMIT = 56 * 1024 * 1024


def _cparams(n_grid):
    return pltpu.CompilerParams(
        dimension_semantics=("arbitrary",) * n_grid, vmem_limit_bytes=VMEM_LIMIT)


def _resident(shape, lead=()):
    lead = tuple(lead)
    nd = len(shape)
    return pl.BlockSpec((None,) * len(lead) + tuple(shape), lambda *_: lead + (0,) * nd,
                        pipeline_mode=pl.Buffered(1))


def _tokens(refs, n_lat_tiles):
    if len(refs) == 1:
        return refs[0][...]
    return jnp.where(pl.program_id(0) < n_lat_tiles, refs[0][...], refs[1][...])


def _layer_norm(z, g, b):
    mu = jnp.mean(z, axis=-1, keepdims=True)
    zc = z - mu
    var = jnp.mean(zc * zc, axis=-1, keepdims=True)
    return zc * lax.rsqrt(var + LN_EPS) * g + b


def _swiglu(u_bf16, win_ref, wout_ref):
    d_ff = wout_ref.shape[0]
    a = jnp.dot(u_bf16, win_ref[:, :d_ff], preferred_element_type=F32)
    v = jnp.dot(u_bf16, win_ref[:, d_ff:], preferred_element_type=F32)
    g = (a * jax.nn.sigmoid(a) * v).astype(BF16)
    return jnp.dot(g, wout_ref[...], preferred_element_type=F32)


def _ffn_sublayer(x, mod, base, win_ref, wout_ref, g, b, alpha):
    shift, scale, gate = mod[base:base + 1], mod[base + 1:base + 2], mod[base + 2:base + 3]
    u = (x * (1.0 + scale) + shift).astype(BF16)
    y = _swiglu(u, win_ref, wout_ref)
    return _layer_norm(alpha * x + (FFN_RES * gate) * y, g, b)


def _mod_body(c_ref, w_ref, b_ref, o_ref):
    c = c_ref[...]
    s = (c * jax.nn.sigmoid(c)).astype(BF16)
    o_ref[0] = jnp.dot(s, w_ref[0].astype(BF16), preferred_element_type=F32) + b_ref[0]


def _modulation(cc, w_mod, b_mod):
    depth, d, nd = w_mod.shape
    r = cc.shape[0]
    out = pl.pallas_call(
        _mod_body,
        grid=(depth, nd // d),
        in_specs=[
            pl.BlockSpec((r, d), lambda i, j: (0, 0)),
            pl.BlockSpec((1, d, d), lambda i, j: (i, 0, j)),
            pl.BlockSpec((1, 1, d), lambda i, j: (i, 0, j)),
        ],
        out_specs=pl.BlockSpec((1, r, d), lambda i, j: (i, 0, j)),
        out_shape=jax.ShapeDtypeStruct((depth, r, nd), F32),
        compiler_params=_cparams(2),
        name="modulation",
    )(cc, w_mod, b_mod.reshape(depth, 1, nd))
    return out.reshape(depth, r, nd // d, d)


def _ffn_body(mod_ref, *refs, alpha, n_x, n_lat_tiles):
    x_refs, (win_ref, wout_ref, g_ref, b_ref, o_ref) = refs[:n_x], refs[n_x:]
    o_ref[...] = _ffn_sublayer(_tokens(x_refs, n_lat_tiles), mod_ref[0], 0, win_ref, wout_ref,
                               g_ref[0:1], b_ref[0:1], alpha)


def _post_body(mod_ref, h_ref, *refs, alpha, n_a, n_lat_tiles):
    a_refs, (wo_ref, win_ref, wout_ref, g_ref, b_ref, o_ref) = refs[:n_a], refs[n_a:]
    mod = mod_ref[0]
    y = jnp.dot(_tokens(a_refs, n_lat_tiles), wo_ref[...], preferred_element_type=F32)
    h = _layer_norm(alpha * h_ref[...] + mod[5:6] * y, g_ref[1:2], b_ref[1:2])
    o_ref[...] = _ffn_sublayer(h, mod, 6, win_ref, wout_ref, g_ref[2:3], b_ref[2:3], alpha)


def _qkv_na_body(mod_ref, h_ref, w_ref, q_ref, k_ref, v_ref):
    mod = mod_ref[0]
    d = h_ref.shape[1]
    u = (h_ref[...] * (1.0 + mod[4:5]) + mod[3:4]).astype(BF16)
    r = jnp.dot(u, w_ref[...], preferred_element_type=F32)
    q_ref[...] = (r[:, :d] * HEAD_DIM ** -0.5).astype(BF16)
    k_ref[...] = r[:, d:2 * d].astype(BF16)
    v_ref[...] = r[:, 2 * d:].astype(BF16)


def _rope_lanes(x, cos, sin_signed, first_half):
    partner = jnp.where(first_half, pltpu.roll(x, LANES - 16, 1), pltpu.roll(x, 16, 1))
    return x * cos + partner * sin_signed


def _qkv_wa_body(mod_ref, h_ref, cos_ref, sin_ref, w_ref, q_ref, k_ref, v_ref):
    mod = mod_ref[0]
    d = h_ref.shape[1]
    nk = k_ref.shape[1]
    u = (h_ref[...] * (1.0 + mod[4:5]) + mod[3:4]).astype(BF16)
    r = jnp.dot(u, w_ref[...], preferred_element_type=F32)
    cos, sin = cos_ref[...], sin_ref[...]
    first_half = lax.broadcasted_iota(jnp.int32, cos.shape, 1) % 32 < 16
    for j in range(d // LANES):
        sl = slice(j * LANES, (j + 1) * LANES)
        q_ref[:, sl] = (_rope_lanes(r[:, sl], cos, sin, first_half) * HEAD_DIM ** -0.5).astype(BF16)
    for j in range(nk // LANES):
        sl = slice(d + j * LANES, d + (j + 1) * LANES)
        k_ref[:, j * LANES:(j + 1) * LANES] = _rope_lanes(r[:, sl], cos, sin, first_half).astype(BF16)
    v_ref[...] = r[:, d + nk:].astype(BF16)


def _mod_spec(layer, n_lat_tiles, tiles_per_seq, ctx_row, d):
    return pl.BlockSpec(
        (None, 1, N_MOD, d),
        lambda t: (layer, jnp.where(t < n_lat_tiles, t // tiles_per_seq, ctx_row), 0, 0))


def _tok_spec(width):
    return pl.BlockSpec((TOKEN_TILE, width), lambda t: (t, 0))


def _split_tok_specs(width, n_lat_tiles):
    return [pl.BlockSpec((TOKEN_TILE, width), lambda t: (jnp.minimum(t, n_lat_tiles - 1), 0)),
            pl.BlockSpec((TOKEN_TILE, width), lambda t: (jnp.maximum(t - n_lat_tiles, 0), 0))]


def _split_heads(x):
    lo = lax.broadcasted_iota(jnp.int32, x.shape, 1) < HEAD_DIM
    zero = jnp.zeros_like(x)
    return jnp.concatenate([jnp.where(lo, x, zero), jnp.where(lo, zero, x)], axis=0)


def _merge_heads(o):
    n = o.shape[0] // 2
    lo = lax.broadcasted_iota(jnp.int32, (n, LANES), 1) < HEAD_DIM
    return jnp.where(lo, o[:n], o[n:])


def _qk(q, k):
    return lax.dot_general(q, k, (((1,), (1,)), ((), ())), preferred_element_type=F32)


def _lane_fold(op, *xs):
    parts = [x[:, i:i + LANES] for x in xs for i in range(0, x.shape[1], LANES)]
    return functools.reduce(op, parts)


def _softmax_parts(s_win, s_ctx, sink):
    m = jnp.max(_lane_fold(jnp.maximum, s_win, s_ctx), axis=-1, keepdims=True)
    if sink is not None:
        m = jnp.maximum(m, sink)
    p_win = jnp.exp(s_win - m)
    p_ctx = jnp.exp(s_ctx - m)
    l = jnp.sum(_lane_fold(jnp.add, p_win, p_ctx), axis=-1, keepdims=True)
    if sink is not None:
        l = l + jnp.exp(sink - m)
    return p_win.astype(BF16), p_ctx.astype(BF16), l


def _pipelined(n, scores, finish):
    pending = [scores(i) for i in range(min(ATTN_LOOKAHEAD, n))]
    for i in range(n):
        if i + ATTN_LOOKAHEAD < n:
            pending.append(scores(i + ATTN_LOOKAHEAD))
        finish(i, *pending.pop(0))


def _na_body(q_ref, k_ref, v_ref, kc_ref, vc_ref, t2_ref, o_ref):
    rows = k_ref.shape[0] // GRID_W
    n_win = NA_KH * GRID_W
    n_rows = q_ref.shape[0] // GRID_W
    r0 = pl.program_id(1) * n_rows
    key_rows = [jnp.clip(r0 + i - NA_KH // 2, 0, rows - NA_KH) for i in range(n_rows)]
    k_offs = [pl.multiple_of(rs * GRID_W, GRID_W) for rs in key_rows]
    shifts = [rs - (r0 + i) + NA_KH - 1 for i, rs in enumerate(key_rows)]
    m_row = 2 * GRID_W

    def scores(p):
        sl = slice(p * LANES, (p + 1) * LANES)
        qs = [_split_heads(q_ref[i * GRID_W:(i + 1) * GRID_W, sl]) for i in range(n_rows)]
        s_win = jnp.concatenate(
            [_qk(qs[i], k_ref[pl.ds(k_offs[i], n_win), sl]) for i in range(n_rows)], axis=0)
        return s_win, _qk(jnp.concatenate(qs, axis=0), kc_ref[:, sl])

    def finish(p, s_win, s_ctx):
        sl = slice(p * LANES, (p + 1) * LANES)
        bias = jnp.concatenate(
            [jnp.concatenate(
                [jnp.concatenate([t2_ref[2 * p, sh + 2 * jj], t2_ref[2 * p + 1, sh + 2 * jj]], axis=0)
                 for jj in range(NA_KH // 2)], axis=1) for sh in shifts], axis=0)
        p_win, p_ctx, l = _softmax_parts(s_win + bias, s_ctx, None)
        o_win = jnp.concatenate(
            [jnp.dot(p_win[i * m_row:(i + 1) * m_row], v_ref[pl.ds(k_offs[i], n_win), sl],
                     preferred_element_type=F32) for i in range(n_rows)], axis=0)
        o = (o_win + jnp.dot(p_ctx, vc_ref[:, sl], preferred_element_type=F32)) / l
        for i in range(n_rows):
            o_ref[i * GRID_W:(i + 1) * GRID_W, sl] = _merge_heads(o[i * m_row:(i + 1) * m_row]).astype(BF16)

    _pipelined(N_HEADS // 2, scores, finish)


def _ctx_body(q_ref, k_ref, v_ref, o_ref):
    for p in range(N_HEADS // 2):
        sl = slice(p * LANES, (p + 1) * LANES)
        qs = _split_heads(q_ref[:, sl])
        s = _qk(qs, k_ref[:, sl])
        m = jnp.max(s, axis=-1, keepdims=True)
        e = jnp.exp(s - m)
        l = jnp.sum(e, axis=-1, keepdims=True)
        o = jnp.dot(e.astype(BF16), v_ref[:, sl], preferred_element_type=F32)
        o_ref[:, sl] = _merge_heads(o / l).astype(BF16)


def _wa_body(sink_ref, q_ref, k_ref, v_ref, kc_ref, vc_ref, o_ref):
    seq = k_ref.shape[0]
    blk = q_ref.shape[0]
    n_win = 3 * blk
    qb = pl.program_id(1)
    ws = pl.multiple_of(jnp.clip((qb - 1) * blk, 0, seq - n_win), blk)
    pos_q = qb * blk + lax.broadcasted_iota(jnp.int32, (blk, n_win), 0)
    pos_k = ws + lax.broadcasted_iota(jnp.int32, (blk, n_win), 1)
    band = jnp.where(jnp.abs(pos_q - pos_k) <= WA_WINDOW, 0.0, NEG_INF).astype(F32)
    band = jnp.concatenate([band, band], axis=0)
    pairs_per_kv = N_HEADS // WA_KV_HEADS // 2

    def scores(p):
        kv = slice(p // pairs_per_kv * LANES, (p // pairs_per_kv + 1) * LANES)
        qs = _split_heads(q_ref[:, p * LANES:(p + 1) * LANES])
        return _qk(qs, k_ref[pl.ds(ws, n_win), kv]), _qk(qs, kc_ref[:, kv])

    def finish(p, s_win, s_ctx):
        kv = slice(p // pairs_per_kv * LANES, (p // pairs_per_kv + 1) * LANES)
        sink = jnp.concatenate([jnp.full((blk, 1), sink_ref[2 * p + j], F32) for j in range(2)], axis=0)
        p_win, p_ctx, l = _softmax_parts(s_win + band, s_ctx, sink)
        o = (jnp.dot(p_win, v_ref[pl.ds(ws, n_win), kv], preferred_element_type=F32)
             + jnp.dot(p_ctx, vc_ref[:, kv], preferred_element_type=F32))
        o_ref[:, p * LANES:(p + 1) * LANES] = _merge_heads(o / l).astype(BF16)

    _pipelined(N_HEADS // 2, scores, finish)


def _na_bias_table(rpb):
    qcol = np.arange(GRID_W)[:, None]
    kcol = np.arange(GRID_W)[None, :]
    win_start = np.clip(qcol - NA_KW // 2, 0, GRID_W - NA_KW)
    valid = (kcol >= win_start) & (kcol < win_start + NA_KW)
    off = np.clip(kcol - qcol, -(NA_KW - 1), NA_KW - 1) + NA_KW - 1
    onehot = (off[None] == np.arange(2 * NA_KW - 1)[:, None, None]).astype(np.float32)
    t = jnp.einsum("hdo,oqk->hdqk", rpb.astype(F32), onehot, precision=lax.Precision.HIGHEST)
    t = jnp.where(valid, t, NEG_INF)
    return jnp.concatenate([t[:, :-1], t[:, 1:]], axis=-1)


def _rope_tables(seq, pad_rows):
    t = jnp.arange(seq, dtype=jnp.int32)
    n = HEAD_DIM // 4
    inv_freq = ROPE_BASE ** (-jnp.arange(n, dtype=F32) / n)
    ang_r = (t // GRID_W).astype(F32)[:, None] * inv_freq[None, :]
    ang_c = (t % GRID_W).astype(F32)[:, None] * inv_freq[None, :]
    ang = jnp.concatenate([ang_r, ang_r, ang_c, ang_c], axis=1)
    sign = np.tile(np.concatenate([-np.ones(n), np.ones(n)]), 2).astype(np.float32)
    cos = jnp.cos(ang)
    sin = jnp.sin(ang) * sign
    cos = jnp.concatenate([cos, jnp.ones((pad_rows, HEAD_DIM), F32)], axis=0)
    sin = jnp.concatenate([sin, jnp.zeros((pad_rows, HEAD_DIM), F32)], axis=0)
    return jnp.tile(cos, (1, LANES // HEAD_DIM)), jnp.tile(sin, (1, LANES // HEAD_DIM))


def _dup_heads(w, n_heads):
    d = w.shape[0]
    w = w.reshape(d, n_heads, 1, HEAD_DIM)
    return jnp.broadcast_to(w, (d, n_heads, LANES // HEAD_DIM, HEAD_DIM)).reshape(d, n_heads * LANES)


def kernel(x, c, ctx, c_ctx, w_mod, b_mod, ln_g, ln_b, ffn_w_in, ffn_w_out,
           na_w_qkv, na_w_o, na_rpb, wa_w_qkv, wa_w_o, wa_sinks):
    bsz, seq, d = x.shape
    n_ctx = ctx.shape[1]
    depth = w_mod.shape[0]
    d_ff = ffn_w_out.shape[2]
    alpha = (2 * depth) ** 0.25
    t_lat, t_ctx = bsz * seq, bsz * n_ctx
    t_all = t_lat + t_ctx
    nt_lat, nt_all = t_lat // TOKEN_TILE, t_all // TOKEN_TILE
    tiles_per_seq = seq // TOKEN_TILE
    dq = N_HEADS * HEAD_DIM
    dkv = WA_KV_HEADS * HEAD_DIM

    mod_rows = 16
    cc = jnp.concatenate([c, c_ctx[None], jnp.zeros((mod_rows - bsz - 1, d), F32)], axis=0)
    mods = _modulation(cc, w_mod, b_mod)

    w_in = ffn_w_in.astype(BF16)
    w_out = ffn_w_out.astype(BF16)
    na_qkv = na_w_qkv.astype(BF16)
    na_o = na_w_o.astype(BF16)
    wa_o = wa_w_o.astype(BF16)
    h_parts = [x.reshape(t_lat, d), ctx.reshape(t_ctx, d)]
    ctx_blk0 = t_lat // n_ctx

    for i in range(depth):
        last = i == depth - 1
        is_na = i % 2 == 0
        j = i // 2
        mod_spec = _mod_spec(i, nt_lat, tiles_per_seq, bsz, d)
        ln_spec = pl.BlockSpec((None, 3, d), lambda t, i=i: (i, 0, 0))

        h_specs = _split_tok_specs(d, nt_lat) if len(h_parts) == 2 else [_tok_spec(d)]
        h = pl.pallas_call(
            functools.partial(_ffn_body, alpha=alpha, n_x=len(h_parts), n_lat_tiles=nt_lat),
            grid=(nt_all,),
            in_specs=[mod_spec, *h_specs, _resident((d, 2 * d_ff), (i, 0)), _resident((d_ff, d), (i, 0)),
                      ln_spec, ln_spec],
            out_specs=_tok_spec(d),
            out_shape=jax.ShapeDtypeStruct((t_all, d), F32),
            compiler_params=_cparams(1),
            name=f"ffn_pre_{i}",
        )(mods, *h_parts, w_in, w_out, ln_g, ln_b)

        if is_na:
            q, k, v = pl.pallas_call(
                _qkv_na_body,
                grid=(nt_all,),
                in_specs=[mod_spec, _tok_spec(d), _resident((d, 3 * dq), (j,))],
                out_specs=[_tok_spec(dq)] * 3,
                out_shape=[jax.ShapeDtypeStruct((t_all, dq), BF16)] * 3,
                compiler_params=_cparams(1),
                name=f"qkv_na_{i}",
            )(mods, h, na_qkv)
            t2 = _na_bias_table(na_rpb[j])
            q_rows = NA_ROWS_PER_STEP * GRID_W
            steps = seq // q_rows
            att = pl.pallas_call(
                _na_body,
                grid=(bsz, steps),
                in_specs=[
                    pl.BlockSpec((q_rows, dq), lambda b, r: (b * steps + r, 0)),
                    pl.BlockSpec((seq, dq), lambda b, r: (b, 0)),
                    pl.BlockSpec((seq, dq), lambda b, r: (b, 0)),
                    pl.BlockSpec((n_ctx, dq), lambda b, r: (ctx_blk0 + b, 0)),
                    pl.BlockSpec((n_ctx, dq), lambda b, r: (ctx_blk0 + b, 0)),
                    _resident(t2.shape),
                ],
                out_specs=pl.BlockSpec((q_rows, dq), lambda b, r: (b * steps + r, 0)),
                out_shape=jax.ShapeDtypeStruct((t_lat, dq), BF16),
                compiler_params=_cparams(2),
                name=f"attn_na_{i}",
            )(q, k, v, k, v, t2)
            w_o, w_o_spec = na_o, _resident((dq, d), (j,))
        else:
            wq = wa_w_qkv[j]
            w_qkv = jnp.concatenate(
                [wq[:, :dq], _dup_heads(wq[:, dq:dq + dkv], WA_KV_HEADS), _dup_heads(wq[:, dq + dkv:], WA_KV_HEADS)],
                axis=1).astype(BF16)
            nkv = WA_KV_HEADS * LANES
            cos, sin = _rope_tables(seq, TOKEN_TILE)
            rope_spec = pl.BlockSpec(
                (TOKEN_TILE, LANES), lambda t: (jnp.where(t < nt_lat, t % tiles_per_seq, tiles_per_seq), 0))
            q, k, v = pl.pallas_call(
                _qkv_wa_body,
                grid=(nt_all,),
                in_specs=[mod_spec, _tok_spec(d), rope_spec, rope_spec, _resident((d, dq + 2 * nkv))],
                out_specs=[_tok_spec(dq), _tok_spec(nkv), _tok_spec(nkv)],
                out_shape=[jax.ShapeDtypeStruct((t_all, dq), BF16),
                           jax.ShapeDtypeStruct((t_all, nkv), BF16),
                           jax.ShapeDtypeStruct((t_all, nkv), BF16)],
                compiler_params=_cparams(1),
                name=f"qkv_wa_{i}",
            )(mods, h, cos, sin, w_qkv)
            blk = WA_WINDOW
            steps = seq // blk
            att = pl.pallas_call(
                _wa_body,
                grid=(bsz, steps),
                in_specs=[
                    pl.BlockSpec(memory_space=pltpu.SMEM),
                    pl.BlockSpec((blk, dq), lambda b, r: (b * steps + r, 0)),
                    pl.BlockSpec((seq, nkv), lambda b, r: (b, 0)),
                    pl.BlockSpec((seq, nkv), lambda b, r: (b, 0)),
                    pl.BlockSpec((n_ctx, nkv), lambda b, r: (ctx_blk0 + b, 0)),
                    pl.BlockSpec((n_ctx, nkv), lambda b, r: (ctx_blk0 + b, 0)),
                ],
                out_specs=pl.BlockSpec((blk, dq), lambda b, r: (b * steps + r, 0)),
                out_shape=jax.ShapeDtypeStruct((t_lat, dq), BF16),
                compiler_params=_cparams(2),
                name=f"attn_wa_{i}",
            )(wa_sinks[j], q, k, v, k, v)
            w_o, w_o_spec = wa_o, _resident((dq, d), (j,))

        att_parts = [att]
        if not last:
            ctx_spec = pl.BlockSpec((n_ctx, dq), lambda b: (ctx_blk0 + b, 0))
            att_parts.append(pl.pallas_call(
                _ctx_body,
                grid=(bsz,),
                in_specs=[ctx_spec] * 3,
                out_specs=pl.BlockSpec((n_ctx, dq), lambda b: (b, 0)),
                out_shape=jax.ShapeDtypeStruct((t_ctx, dq), BF16),
                compiler_params=_cparams(1),
                name=f"attn_ctx_{i}",
            )(q, k, v))

        n_tiles = nt_lat if last else nt_all
        att_specs = _split_tok_specs(dq, nt_lat) if len(att_parts) == 2 else [_tok_spec(dq)]
        h = pl.pallas_call(
            functools.partial(_post_body, alpha=alpha, n_a=len(att_parts), n_lat_tiles=nt_lat),
            grid=(n_tiles,),
            in_specs=[mod_spec, _tok_spec(d), *att_specs, w_o_spec,
                      _resident((d, 2 * d_ff), (i, 1)), _resident((d_ff, d), (i, 1)), ln_spec, ln_spec],
            out_specs=_tok_spec(d),
            out_shape=jax.ShapeDtypeStruct((n_tiles * TOKEN_TILE, d), F32),
            compiler_params=_cparams(1),
            name=f"post_{i}",
        )(mods, h, *att_parts, w_o, w_in, w_out, ln_g, ln_b)
        h_parts = [h]

    return h.reshape(bsz, seq, d)
```

```python
import functools

import numpy as np
import jax
import jax.numpy as jnp
from jax import lax
from jax.experimental import pallas as pl
from jax.experimental.pallas import tpu as pltpu

F32 = jnp.float32
BF16 = jnp.bfloat16

GRID_W = 64
N_HEADS = 16
HEAD_DIM = 64
NA_KH = 8
NA_KW = 16
WA_KV_HEADS = 4
WA_WINDOW = 128
FFN_RES = 0.5
ROPE_BASE = 10000.0
N_MOD = 9
LN_EPS = 1e-5
NEG_INF = -1e30

LANES = 128
TOKEN_TILE = 512
NA_ROWS_PER_STEP = 8
WA_BLOCKS_PER_STEP = 4
ATTN_LOOKAHEAD = 3
LOG2E = 1.4426950408889634
Q_SCALE = HEAD_DIM ** -0.5 * LOG2E
VMEM_LIMIT = 56 * 1024 * 1024


def _cparams(n_grid):
    return pltpu.CompilerParams(
        dimension_semantics=("arbitrary",) * n_grid, vmem_limit_bytes=VMEM_LIMIT)


def _resident(shape, lead=()):
    lead = tuple(lead)
    nd = len(shape)
    return pl.BlockSpec((None,) * len(lead) + tuple(shape), lambda *_: lead + (0,) * nd,
                        pipeline_mode=pl.Buffered(1))


def _tokens(refs, n_lat_tiles):
    if len(refs) == 1:
        return refs[0][...]
    return jnp.where(pl.program_id(0) < n_lat_tiles, refs[0][...], refs[1][...])


def _layer_norm(z, g, b):
    mu = jnp.mean(z, axis=-1, keepdims=True)
    zc = z - mu
    var = jnp.mean(zc * zc, axis=-1, keepdims=True)
    return zc * lax.rsqrt(var + LN_EPS) * g + b


def _swiglu(u_bf16, win_ref, wout_ref):
    d_ff = wout_ref.shape[0]
    a = jnp.dot(u_bf16, win_ref[:, :d_ff], preferred_element_type=F32)
    v = jnp.dot(u_bf16, win_ref[:, d_ff:], preferred_element_type=F32)
    g = (a * jax.nn.sigmoid(a) * v).astype(BF16)
    return jnp.dot(g, wout_ref[...], preferred_element_type=F32)


def _ffn_sublayer(x, mod, base, win_ref, wout_ref, g, b, alpha):
    shift, scale, gate = mod[base:base + 1], mod[base + 1:base + 2], mod[base + 2:base + 3]
    u = (x * (1.0 + scale) + shift).astype(BF16)
    y = _swiglu(u, win_ref, wout_ref)
    return _layer_norm(alpha * x + (FFN_RES * gate) * y, g, b)


def _mod_body(c_ref, w_ref, b_ref, o_ref):
    c = c_ref[...]
    s = (c * jax.nn.sigmoid(c)).astype(BF16)
    o_ref[0] = jnp.dot(s, w_ref[0].astype(BF16), preferred_element_type=F32) + b_ref[0]


def _modulation(cc, w_mod, b_mod):
    depth, d, nd = w_mod.shape
    r = cc.shape[0]
    out = pl.pallas_call(
        _mod_body,
        grid=(depth, nd // d),
        in_specs=[
            pl.BlockSpec((r, d), lambda i, j: (0, 0)),
            pl.BlockSpec((1, d, d), lambda i, j: (i, 0, j)),
            pl.BlockSpec((1, 1, d), lambda i, j: (i, 0, j)),
        ],
        out_specs=pl.BlockSpec((1, r, d), lambda i, j: (i, 0, j)),
        out_shape=jax.ShapeDtypeStruct((depth, r, nd), F32),
        compiler_params=_cparams(2),
        name="modulation",
    )(cc, w_mod, b_mod.reshape(depth, 1, nd))
    return out.reshape(depth, r, nd // d, d)


def _ffn_body(mod_ref, *refs, alpha, n_x, n_lat_tiles):
    x_refs, (win_ref, wout_ref, g_ref, b_ref, o_ref) = refs[:n_x], refs[n_x:]
    o_ref[...] = _ffn_sublayer(_tokens(x_refs, n_lat_tiles), mod_ref[0], 0, win_ref, wout_ref,
                               g_ref[0:1], b_ref[0:1], alpha)


def _post_body(mod_ref, h_ref, *refs, alpha, n_a, n_lat_tiles):
    a_refs, (wo_ref, win_ref, wout_ref, g_ref, b_ref, o_ref) = refs[:n_a], refs[n_a:]
    mod = mod_ref[0]
    y = jnp.dot(_tokens(a_refs, n_lat_tiles), wo_ref[...], preferred_element_type=F32)
    h = _layer_norm(alpha * h_ref[...] + mod[5:6] * y, g_ref[1:2], b_ref[1:2])
    o_ref[...] = _ffn_sublayer(h, mod, 6, win_ref, wout_ref, g_ref[2:3], b_ref[2:3], alpha)


def _qkv_na_body(mod_ref, h_ref, w_ref, q_ref, k_ref, v_ref):
    mod = mod_ref[0]
    d = h_ref.shape[1]
    u = (h_ref[...] * (1.0 + mod[4:5]) + mod[3:4]).astype(BF16)
    r = jnp.dot(u, w_ref[...], preferred_element_type=F32)
    q_ref[...] = (r[:, :d] * Q_SCALE).astype(BF16)
    k_ref[...] = r[:, d:2 * d].astype(BF16)
    v_ref[...] = r[:, 2 * d:].astype(BF16)


def _rope_lanes(x, cos, sin_signed, first_half):
    partner = jnp.where(first_half, pltpu.roll(x, LANES - 16, 1), pltpu.roll(x, 16, 1))
    return x * cos + partner * sin_signed


def _qkv_wa_body(mod_ref, h_ref, cos_ref, sin_ref, w_ref, q_ref, k_ref, v_ref):
    mod = mod_ref[0]
    d = h_ref.shape[1]
    nk = k_ref.shape[1]
    u = (h_ref[...] * (1.0 + mod[4:5]) + mod[3:4]).astype(BF16)
    r = jnp.dot(u, w_ref[...], preferred_element_type=F32)
    cos, sin = cos_ref[...], sin_ref[...]
    first_half = lax.broadcasted_iota(jnp.int32, cos.shape, 1) % 32 < 16
    for j in range(d // LANES):
        sl = slice(j * LANES, (j + 1) * LANES)
        q_ref[:, sl] = (_rope_lanes(r[:, sl], cos, sin, first_half) * Q_SCALE).astype(BF16)
    for j in range(nk // LANES):
        sl = slice(d + j * LANES, d + (j + 1) * LANES)
        k_ref[:, j * LANES:(j + 1) * LANES] = _rope_lanes(r[:, sl], cos, sin, first_half).astype(BF16)
    v_ref[...] = r[:, d + nk:].astype(BF16)


def _mod_spec(layer, n_lat_tiles, tiles_per_seq, ctx_row, d):
    return pl.BlockSpec(
        (None, 1, N_MOD, d),
        lambda t: (layer, jnp.where(t < n_lat_tiles, t // tiles_per_seq, ctx_row), 0, 0))


def _tok_spec(width):
    return pl.BlockSpec((TOKEN_TILE, width), lambda t: (t, 0))


def _split_tok_specs(width, n_lat_tiles):
    return [pl.BlockSpec((TOKEN_TILE, width), lambda t: (jnp.minimum(t, n_lat_tiles - 1), 0)),
            pl.BlockSpec((TOKEN_TILE, width), lambda t: (jnp.maximum(t - n_lat_tiles, 0), 0))]


def _split_heads(x):
    lo = lax.broadcasted_iota(jnp.int32, x.shape, 1) < HEAD_DIM
    zero = jnp.zeros_like(x)
    return jnp.concatenate([jnp.where(lo, x, zero), jnp.where(lo, zero, x)], axis=0)


def _merge_heads(o):
    n = o.shape[0] // 2
    lo = lax.broadcasted_iota(jnp.int32, (n, LANES), 1) < HEAD_DIM
    return jnp.where(lo, o[:n], o[n:])


def _qk(q, k):
    return lax.dot_general(q, k, (((1,), (1,)), ((), ())), preferred_element_type=F32)


def _lane_fold(op, *xs):
    parts = [x[:, i:i + LANES] for x in xs for i in range(0, x.shape[1], LANES)]
    return functools.reduce(op, parts)


def _softmax_parts(s_win, s_ctx, sink):
    m = jnp.max(_lane_fold(jnp.maximum, s_win, s_ctx), axis=-1, keepdims=True)
    if sink is not None:
        m = jnp.maximum(m, sink)
    p_win = jnp.exp2(s_win - m)
    p_ctx = jnp.exp2(s_ctx - m)
    l = jnp.sum(_lane_fold(jnp.add, p_win, p_ctx), axis=-1, keepdims=True)
    if sink is not None:
        l = l + jnp.exp2(sink - m)
    return p_win.astype(BF16), p_ctx.astype(BF16), l


def _pipelined(n, scores, finish):
    pending = [scores(i) for i in range(min(ATTN_LOOKAHEAD, n))]
    for i in range(n):
        if i + ATTN_LOOKAHEAD < n:
            pending.append(scores(i + ATTN_LOOKAHEAD))
        finish(i, *pending.pop(0))


def _na_body(q_ref, k_ref, v_ref, kc_ref, vc_ref, t2_ref, o_ref):
    rows = k_ref.shape[0] // GRID_W
    n_win = NA_KH * GRID_W
    n_rows = q_ref.shape[0] // GRID_W
    r0 = pl.program_id(1) * n_rows
    key_rows = [jnp.clip(r0 + i - NA_KH // 2, 0, rows - NA_KH) for i in range(n_rows)]
    k_offs = [pl.multiple_of(rs * GRID_W, GRID_W) for rs in key_rows]
    shifts = [rs - (r0 + i) + NA_KH - 1 for i, rs in enumerate(key_rows)]
    m_row = 2 * GRID_W

    def scores(p):
        sl = slice(p * LANES, (p + 1) * LANES)
        qs = [_split_heads(q_ref[i * GRID_W:(i + 1) * GRID_W, sl]) for i in range(n_rows)]
        s_win = jnp.concatenate(
            [_qk(qs[i], k_ref[pl.ds(k_offs[i], n_win), sl]) for i in range(n_rows)], axis=0)
        return s_win, _qk(jnp.concatenate(qs, axis=0), kc_ref[:, sl])

    def finish(p, s_win, s_ctx):
        sl = slice(p * LANES, (p + 1) * LANES)
        bias = jnp.concatenate(
            [jnp.concatenate(
                [jnp.concatenate([t2_ref[2 * p, sh + 2 * jj], t2_ref[2 * p + 1, sh + 2 * jj]], axis=0)
                 for jj in range(NA_KH // 2)], axis=1) for sh in shifts], axis=0)
        p_win, p_ctx, l = _softmax_parts(s_win + bias, s_ctx, None)
        o_win = jnp.concatenate(
            [jnp.dot(p_win[i * m_row:(i + 1) * m_row], v_ref[pl.ds(k_offs[i], n_win), sl],
                     preferred_element_type=F32) for i in range(n_rows)], axis=0)
        o = (o_win + jnp.dot(p_ctx, vc_ref[:, sl], preferred_element_type=F32)) / l
        for i in range(n_rows):
            o_ref[i * GRID_W:(i + 1) * GRID_W, sl] = _merge_heads(o[i * m_row:(i + 1) * m_row]).astype(BF16)

    _pipelined(N_HEADS // 2, scores, finish)


def _ctx_body(q_ref, k_ref, v_ref, o_ref):
    for p in range(N_HEADS // 2):
        sl = slice(p * LANES, (p + 1) * LANES)
        qs = _split_heads(q_ref[:, sl])
        s = _qk(qs, k_ref[:, sl])
        m = jnp.max(s, axis=-1, keepdims=True)
        e = jnp.exp2(s - m)
        l = jnp.sum(e, axis=-1, keepdims=True)
        o = jnp.dot(e.astype(BF16), v_ref[:, sl], preferred_element_type=F32)
        o_ref[:, sl] = _merge_heads(o / l).astype(BF16)


def _wa_body(sink_ref, q_ref, k_ref, v_ref, kc_ref, vc_ref, o_ref):
    seq = k_ref.shape[0]
    blk = WA_WINDOW
    n_blk = q_ref.shape[0] // blk
    n_win = 3 * blk
    qb0 = pl.program_id(1) * n_blk
    starts = [pl.multiple_of(jnp.clip((qb0 + b - 1) * blk, 0, seq - n_win), blk) for b in range(n_blk)]
    d_row = lax.broadcasted_iota(jnp.int32, (blk, n_win), 0) - lax.broadcasted_iota(jnp.int32, (blk, n_win), 1)
    bands = []
    for b in range(n_blk):
        dist = d_row + ((qb0 + b) * blk - starts[b])
        band = jnp.where(jnp.abs(dist) <= WA_WINDOW, 0.0, NEG_INF).astype(F32)
        bands += [band, band]
    band = jnp.concatenate(bands, axis=0)
    pairs_per_kv = N_HEADS // WA_KV_HEADS // 2
    m_blk = 2 * blk

    def scores(p):
        kv = slice(p // pairs_per_kv * LANES, (p // pairs_per_kv + 1) * LANES)
        qs = [_split_heads(q_ref[b * blk:(b + 1) * blk, p * LANES:(p + 1) * LANES]) for b in range(n_blk)]
        s_win = jnp.concatenate(
            [_qk(qs[b], k_ref[pl.ds(starts[b], n_win), kv]) for b in range(n_blk)], axis=0)
        return s_win, _qk(jnp.concatenate(qs, axis=0), kc_ref[:, kv])

    def finish(p, s_win, s_ctx):
        kv = slice(p // pairs_per_kv * LANES, (p // pairs_per_kv + 1) * LANES)
        sink = jnp.concatenate(
            [jnp.full((blk, 1), sink_ref[2 * p + j] * LOG2E, F32) for j in range(2)] * n_blk, axis=0)
        p_win, p_ctx, l = _softmax_parts(s_win + band, s_ctx, sink)
        o_win = jnp.concatenate(
            [jnp.dot(p_win[b * m_blk:(b + 1) * m_blk], v_ref[pl.ds(starts[b], n_win), kv],
                     preferred_element_type=F32) for b in range(n_blk)], axis=0)
        o = (o_win + jnp.dot(p_ctx, vc_ref[:, kv], preferred_element_type=F32)) / l
        for b in range(n_blk):
            o_ref[b * blk:(b + 1) * blk, p * LANES:(p + 1) * LANES] = _merge_heads(
                o[b * m_blk:(b + 1) * m_blk]).astype(BF16)

    _pipelined(N_HEADS // 2, scores, finish)


def _na_bias_table(rpb):
    qcol = np.arange(GRID_W)[:, None]
    kcol = np.arange(GRID_W)[None, :]
    win_start = np.clip(qcol - NA_KW // 2, 0, GRID_W - NA_KW)
    valid = (kcol >= win_start) & (kcol < win_start + NA_KW)
    off = np.clip(kcol - qcol, -(NA_KW - 1), NA_KW - 1) + NA_KW - 1
    onehot = (off[None] == np.arange(2 * NA_KW - 1)[:, None, None]).astype(np.float32)
    t = jnp.einsum("hdo,oqk->hdqk", rpb.astype(F32), onehot, precision=lax.Precision.HIGHEST)
    t = jnp.where(valid, t * LOG2E, NEG_INF)
    return jnp.concatenate([t[:, :-1], t[:, 1:]], axis=-1)


def _rope_tables(seq, pad_rows):
    t = jnp.arange(seq, dtype=jnp.int32)
    n = HEAD_DIM // 4
    inv_freq = ROPE_BASE ** (-jnp.arange(n, dtype=F32) / n)
    ang_r = (t // GRID_W).astype(F32)[:, None] * inv_freq[None, :]
    ang_c = (t % GRID_W).astype(F32)[:, None] * inv_freq[None, :]
    ang = jnp.concatenate([ang_r, ang_r, ang_c, ang_c], axis=1)
    sign = np.tile(np.concatenate([-np.ones(n), np.ones(n)]), 2).astype(np.float32)
    cos = jnp.cos(ang)
    sin = jnp.sin(ang) * sign
    cos = jnp.concatenate([cos, jnp.ones((pad_rows, HEAD_DIM), F32)], axis=0)
    sin = jnp.concatenate([sin, jnp.zeros((pad_rows, HEAD_DIM), F32)], axis=0)
    return jnp.tile(cos, (1, LANES // HEAD_DIM)), jnp.tile(sin, (1, LANES // HEAD_DIM))


def _dup_heads(w, n_heads):
    d = w.shape[0]
    w = w.reshape(d, n_heads, 1, HEAD_DIM)
    return jnp.broadcast_to(w, (d, n_heads, LANES // HEAD_DIM, HEAD_DIM)).reshape(d, n_heads * LANES)


def kernel(x, c, ctx, c_ctx, w_mod, b_mod, ln_g, ln_b, ffn_w_in, ffn_w_out,
           na_w_qkv, na_w_o, na_rpb, wa_w_qkv, wa_w_o, wa_sinks):
    bsz, seq, d = x.shape
    n_ctx = ctx.shape[1]
    depth = w_mod.shape[0]
    d_ff = ffn_w_out.shape[2]
    alpha = (2 * depth) ** 0.25
    t_lat, t_ctx = bsz * seq, bsz * n_ctx
    t_all = t_lat + t_ctx
    nt_lat, nt_all = t_lat // TOKEN_TILE, t_all // TOKEN_TILE
    tiles_per_seq = seq // TOKEN_TILE
    dq = N_HEADS * HEAD_DIM
    dkv = WA_KV_HEADS * HEAD_DIM

    mod_rows = 16
    cc = jnp.concatenate([c, c_ctx[None], jnp.zeros((mod_rows - bsz - 1, d), F32)], axis=0)
    mods = _modulation(cc, w_mod, b_mod)

    w_in = ffn_w_in.astype(BF16)
    w_out = ffn_w_out.astype(BF16)
    na_qkv = na_w_qkv.astype(BF16)
    na_o = na_w_o.astype(BF16)
    wa_o = wa_w_o.astype(BF16)
    h_parts = [x.reshape(t_lat, d), ctx.reshape(t_ctx, d)]
    ctx_blk0 = t_lat // n_ctx

    for i in range(depth):
        last = i == depth - 1
        is_na = i % 2 == 0
        j = i // 2
        mod_spec = _mod_spec(i, nt_lat, tiles_per_seq, bsz, d)
        ln_spec = pl.BlockSpec((None, 3, d), lambda t, i=i: (i, 0, 0))

        h_specs = _split_tok_specs(d, nt_lat) if len(h_parts) == 2 else [_tok_spec(d)]
        h = pl.pallas_call(
            functools.partial(_ffn_body, alpha=alpha, n_x=len(h_parts), n_lat_tiles=nt_lat),
            grid=(nt_all,),
            in_specs=[mod_spec, *h_specs, _resident((d, 2 * d_ff), (i, 0)), _resident((d_ff, d), (i, 0)),
                      ln_spec, ln_spec],
            out_specs=_tok_spec(d),
            out_shape=jax.ShapeDtypeStruct((t_all, d), F32),
            compiler_params=_cparams(1),
            name=f"ffn_pre_{i}",
        )(mods, *h_parts, w_in, w_out, ln_g, ln_b)

        if is_na:
            q, k, v = pl.pallas_call(
                _qkv_na_body,
                grid=(nt_all,),
                in_specs=[mod_spec, _tok_spec(d), _resident((d, 3 * dq), (j,))],
                out_specs=[_tok_spec(dq)] * 3,
                out_shape=[jax.ShapeDtypeStruct((t_all, dq), BF16)] * 3,
                compiler_params=_cparams(1),
                name=f"qkv_na_{i}",
            )(mods, h, na_qkv)
            t2 = _na_bias_table(na_rpb[j])
            q_rows = NA_ROWS_PER_STEP * GRID_W
            steps = seq // q_rows
            att = pl.pallas_call(
                _na_body,
                grid=(bsz, steps),
                in_specs=[
                    pl.BlockSpec((q_rows, dq), lambda b, r: (b * steps + r, 0)),
                    pl.BlockSpec((seq, dq), lambda b, r: (b, 0)),
                    pl.BlockSpec((seq, dq), lambda b, r: (b, 0)),
                    pl.BlockSpec((n_ctx, dq), lambda b, r: (ctx_blk0 + b, 0)),
                    pl.BlockSpec((n_ctx, dq), lambda b, r: (ctx_blk0 + b, 0)),
                    _resident(t2.shape),
                ],
                out_specs=pl.BlockSpec((q_rows, dq), lambda b, r: (b * steps + r, 0)),
                out_shape=jax.ShapeDtypeStruct((t_lat, dq), BF16),
                compiler_params=_cparams(2),
                name=f"attn_na_{i}",
            )(q, k, v, k, v, t2)
            w_o, w_o_spec = na_o, _resident((dq, d), (j,))
        else:
            wq = wa_w_qkv[j]
            w_qkv = jnp.concatenate(
                [wq[:, :dq], _dup_heads(wq[:, dq:dq + dkv], WA_KV_HEADS), _dup_heads(wq[:, dq + dkv:], WA_KV_HEADS)],
                axis=1).astype(BF16)
            nkv = WA_KV_HEADS * LANES
            cos, sin = _rope_tables(seq, TOKEN_TILE)
            rope_spec = pl.BlockSpec(
                (TOKEN_TILE, LANES), lambda t: (jnp.where(t < nt_lat, t % tiles_per_seq, tiles_per_seq), 0))
            q, k, v = pl.pallas_call(
                _qkv_wa_body,
                grid=(nt_all,),
                in_specs=[mod_spec, _tok_spec(d), rope_spec, rope_spec, _resident((d, dq + 2 * nkv))],
                out_specs=[_tok_spec(dq), _tok_spec(nkv), _tok_spec(nkv)],
                out_shape=[jax.ShapeDtypeStruct((t_all, dq), BF16),
                           jax.ShapeDtypeStruct((t_all, nkv), BF16),
                           jax.ShapeDtypeStruct((t_all, nkv), BF16)],
                compiler_params=_cparams(1),
                name=f"qkv_wa_{i}",
            )(mods, h, cos, sin, w_qkv)
            blk = WA_BLOCKS_PER_STEP * WA_WINDOW
            steps = seq // blk
            att = pl.pallas_call(
                _wa_body,
                grid=(bsz, steps),
                in_specs=[
                    pl.BlockSpec(memory_space=pltpu.SMEM),
                    pl.BlockSpec((blk, dq), lambda b, r: (b * steps + r, 0)),
                    pl.BlockSpec((seq, nkv), lambda b, r: (b, 0)),
                    pl.BlockSpec((seq, nkv), lambda b, r: (b, 0)),
                    pl.BlockSpec((n_ctx, nkv), lambda b, r: (ctx_blk0 + b, 0)),
                    pl.BlockSpec((n_ctx, nkv), lambda b, r: (ctx_blk0 + b, 0)),
                ],
                out_specs=pl.BlockSpec((blk, dq), lambda b, r: (b * steps + r, 0)),
                out_shape=jax.ShapeDtypeStruct((t_lat, dq), BF16),
                compiler_params=_cparams(2),
                name=f"attn_wa_{i}",
            )(wa_sinks[j], q, k, v, k, v)
            w_o, w_o_spec = wa_o, _resident((dq, d), (j,))

        att_parts = [att]
        if not last:
            ctx_spec = pl.BlockSpec((n_ctx, dq), lambda b: (ctx_blk0 + b, 0))
            att_parts.append(pl.pallas_call(
                _ctx_body,
                grid=(bsz,),
                in_specs=[ctx_spec] * 3,
                out_specs=pl.BlockSpec((n_ctx, dq), lambda b: (b, 0)),
                out_shape=jax.ShapeDtypeStruct((t_ctx, dq), BF16),
                compiler_params=_cparams(1),
                name=f"attn_ctx_{i}",
            )(q, k, v))

        n_tiles = nt_lat if last else nt_all
        att_specs = _split_tok_specs(dq, nt_lat) if len(att_parts) == 2 else [_tok_spec(dq)]
        h = pl.pallas_call(
            functools.partial(_post_body, alpha=alpha, n_a=len(att_parts), n_lat_tiles=nt_lat),
            grid=(n_tiles,),
            in_specs=[mod_spec, _tok_spec(d), *att_specs, w_o_spec,
                      _resident((d, 2 * d_ff), (i, 1)), _resident((d_ff, d), (i, 1)), ln_spec, ln_spec],
            out_specs=_tok_spec(d),
            out_shape=jax.ShapeDtypeStruct((n_tiles * TOKEN_TILE, d), F32),
            compiler_params=_cparams(1),
            name=f"post_{i}",
        )(mods, h, *att_parts, w_o, w_in, w_out, ln_g, ln_b)
        h_parts = [h]

    return h.reshape(bsz, seq, d)
```

```python
import functools

import numpy as np
import jax
import jax.numpy as jnp
from jax import lax
from jax.experimental import pallas as pl
from jax.experimental.pallas import tpu as pltpu

F32 = jnp.float32
BF16 = jnp.bfloat16

GRID_W = 64
N_HEADS = 16
HEAD_DIM = 64
NA_KH = 8
NA_KW = 16
WA_KV_HEADS = 4
WA_WINDOW = 128
FFN_RES = 0.5
ROPE_BASE = 10000.0
N_MOD = 9
LN_EPS = 1e-5
NEG_INF = -1e30

LANES = 128
TOKEN_TILE = 1024
ROW_CHUNK = 256
NA_ROWS_PER_STEP = 8
WA_BLOCKS_PER_STEP = 4
ATTN_LOOKAHEAD = 3
LOG2E = 1.4426950408889634
Q_SCALE = HEAD_DIM ** -0.5 * LOG2E
VMEM_LIMIT = 56 * 1024 * 1024


def _cparams(n_grid):
    return pltpu.CompilerParams(
        dimension_semantics=("arbitrary",) * n_grid, vmem_limit_bytes=VMEM_LIMIT)


def _resident(shape, lead=()):
    lead = tuple(lead)
    nd = len(shape)
    return pl.BlockSpec((None,) * len(lead) + tuple(shape), lambda *_: lead + (0,) * nd,
                        pipeline_mode=pl.Buffered(1))


def _tokens(refs, n_lat_tiles, rows):
    if len(refs) == 1:
        return refs[0][rows, :]
    return jnp.where(pl.program_id(0) < n_lat_tiles, refs[0][rows, :], refs[1][rows, :])


def _row_chunks(n):
    return [slice(i, i + ROW_CHUNK) for i in range(0, n, ROW_CHUNK)]


def _layer_norm(z, g, b):
    mu = jnp.mean(z, axis=-1, keepdims=True)
    zc = z - mu
    var = jnp.mean(zc * zc, axis=-1, keepdims=True)
    return zc * lax.rsqrt(var + LN_EPS) * g + b


def _swiglu(u_bf16, win_ref, wout_ref):
    d_ff = wout_ref.shape[0]
    a = jnp.dot(u_bf16, win_ref[:, :d_ff], preferred_element_type=F32)
    v = jnp.dot(u_bf16, win_ref[:, d_ff:], preferred_element_type=F32)
    g = (a * jax.nn.sigmoid(a) * v).astype(BF16)
    return jnp.dot(g, wout_ref[...], preferred_element_type=F32)


def _ffn_sublayer(x, mod, base, win_ref, wout_ref, g, b, alpha):
    shift, scale, gate = mod[base:base + 1], mod[base + 1:base + 2], mod[base + 2:base + 3]
    u = (x * (1.0 + scale) + shift).astype(BF16)
    y = _swiglu(u, win_ref, wout_ref)
    return _layer_norm(alpha * x + (FFN_RES * gate) * y, g, b)


def _mod_body(c_ref, w_ref, b_ref, o_ref):
    c = c_ref[...]
    s = (c * jax.nn.sigmoid(c)).astype(BF16)
    o_ref[0] = jnp.dot(s, w_ref[0].astype(BF16), preferred_element_type=F32) + b_ref[0]


def _modulation(cc, w_mod, b_mod):
    depth, d, nd = w_mod.shape
    r = cc.shape[0]
    out = pl.pallas_call(
        _mod_body,
        grid=(depth, nd // d),
        in_specs=[
            pl.BlockSpec((r, d), lambda i, j: (0, 0)),
            pl.BlockSpec((1, d, d), lambda i, j: (i, 0, j)),
            pl.BlockSpec((1, 1, d), lambda i, j: (i, 0, j)),
        ],
        out_specs=pl.BlockSpec((1, r, d), lambda i, j: (i, 0, j)),
        out_shape=jax.ShapeDtypeStruct((depth, r, nd), F32),
        compiler_params=_cparams(2),
        name="modulation",
    )(cc, w_mod, b_mod.reshape(depth, 1, nd))
    return out.reshape(depth, r, nd // d, d)


def _ffn_body(mod_ref, *refs, alpha, n_x, n_lat_tiles):
    x_refs, (win_ref, wout_ref, g_ref, b_ref, o_ref) = refs[:n_x], refs[n_x:]
    for rows in _row_chunks(o_ref.shape[0]):
        o_ref[rows, :] = _ffn_sublayer(_tokens(x_refs, n_lat_tiles, rows), mod_ref[0], 0, win_ref, wout_ref,
                                       g_ref[0:1], b_ref[0:1], alpha)


def _post_body(mod_ref, h_ref, *refs, alpha, n_a, n_lat_tiles):
    a_refs, (wo_ref, win_ref, wout_ref, g_ref, b_ref, o_ref) = refs[:n_a], refs[n_a:]
    mod = mod_ref[0]
    for rows in _row_chunks(o_ref.shape[0]):
        y = jnp.dot(_tokens(a_refs, n_lat_tiles, rows), wo_ref[...], preferred_element_type=F32)
        h = _layer_norm(alpha * h_ref[rows, :] + mod[5:6] * y, g_ref[1:2], b_ref[1:2])
        o_ref[rows, :] = _ffn_sublayer(h, mod, 6, win_ref, wout_ref, g_ref[2:3], b_ref[2:3], alpha)


def _qkv_na_body(mod_ref, h_ref, w_ref, q_ref, k_ref, v_ref):
    mod = mod_ref[0]
    d = h_ref.shape[1]
    u = (h_ref[...] * (1.0 + mod[4:5]) + mod[3:4]).astype(BF16)
    r = jnp.dot(u, w_ref[...], preferred_element_type=F32)
    q_ref[...] = (r[:, :d] * Q_SCALE).astype(BF16)
    k_ref[...] = r[:, d:2 * d].astype(BF16)
    v_ref[...] = r[:, 2 * d:].astype(BF16)


def _rope_lanes(x, cos, sin_signed, first_half):
    partner = jnp.where(first_half, pltpu.roll(x, LANES - 16, 1), pltpu.roll(x, 16, 1))
    return x * cos + partner * sin_signed


def _qkv_wa_body(mod_ref, h_ref, cos_ref, sin_ref, w_ref, q_ref, k_ref, v_ref):
    mod = mod_ref[0]
    d = h_ref.shape[1]
    nk = k_ref.shape[1]
    u = (h_ref[...] * (1.0 + mod[4:5]) + mod[3:4]).astype(BF16)
    r = jnp.dot(u, w_ref[...], preferred_element_type=F32)
    cos, sin = cos_ref[...], sin_ref[...]
    first_half = lax.broadcasted_iota(jnp.int32, cos.shape, 1) % 32 < 16
    for j in range(d // LANES):
        sl = slice(j * LANES, (j + 1) * LANES)
        q_ref[:, sl] = (_rope_lanes(r[:, sl], cos, sin, first_half) * Q_SCALE).astype(BF16)
    for j in range(nk // LANES):
        sl = slice(d + j * LANES, d + (j + 1) * LANES)
        k_ref[:, j * LANES:(j + 1) * LANES] = _rope_lanes(r[:, sl], cos, sin, first_half).astype(BF16)
    v_ref[...] = r[:, d + nk:].astype(BF16)


def _mod_spec(layer, n_lat_tiles, tiles_per_seq, ctx_row, d):
    return pl.BlockSpec(
        (None, 1, N_MOD, d),
        lambda t: (layer, jnp.where(t < n_lat_tiles, t // tiles_per_seq, ctx_row), 0, 0))


def _tok_spec(width):
    return pl.BlockSpec((TOKEN_TILE, width), lambda t: (t, 0))


def _split_tok_specs(width, n_lat_tiles):
    return [pl.BlockSpec((TOKEN_TILE, width), lambda t: (jnp.minimum(t, n_lat_tiles - 1), 0)),
            pl.BlockSpec((TOKEN_TILE, width), lambda t: (jnp.maximum(t - n_lat_tiles, 0), 0))]


def _split_heads(x):
    lo = lax.broadcasted_iota(jnp.int32, x.shape, 1) < HEAD_DIM
    zero = jnp.zeros_like(x)
    return jnp.concatenate([jnp.where(lo, x, zero), jnp.where(lo, zero, x)], axis=0)


def _merge_heads(o):
    n = o.shape[0] // 2
    lo = lax.broadcasted_iota(jnp.int32, (n, LANES), 1) < HEAD_DIM
    return jnp.where(lo, o[:n], o[n:])


def _qk(q, k):
    return lax.dot_general(q, k, (((1,), (1,)), ((), ())), preferred_element_type=F32)


def _lane_fold(op, *xs):
    parts = [x[:, i:i + LANES] for x in xs for i in range(0, x.shape[1], LANES)]
    return functools.reduce(op, parts)


def _softmax_parts(s_win, s_ctx, sink):
    m = jnp.max(_lane_fold(jnp.maximum, s_win, s_ctx), axis=-1, keepdims=True)
    if sink is not None:
        m = jnp.maximum(m, sink)
    p_win = jnp.exp2(s_win - m)
    p_ctx = jnp.exp2(s_ctx - m)
    l = jnp.sum(_lane_fold(jnp.add, p_win, p_ctx), axis=-1, keepdims=True)
    if sink is not None:
        l = l + jnp.exp2(sink - m)
    return p_win.astype(BF16), p_ctx.astype(BF16), l


def _pipelined(n, scores, finish):
    pending = [scores(i) for i in range(min(ATTN_LOOKAHEAD, n))]
    for i in range(n):
        if i + ATTN_LOOKAHEAD < n:
            pending.append(scores(i + ATTN_LOOKAHEAD))
        finish(i, *pending.pop(0))


def _na_body(q_ref, k_ref, v_ref, kc_ref, vc_ref, t2_ref, o_ref):
    rows = k_ref.shape[0] // GRID_W
    n_win = NA_KH * GRID_W
    n_rows = q_ref.shape[0] // GRID_W
    r0 = pl.program_id(1) * n_rows
    key_rows = [jnp.clip(r0 + i - NA_KH // 2, 0, rows - NA_KH) for i in range(n_rows)]
    k_offs = [pl.multiple_of(rs * GRID_W, GRID_W) for rs in key_rows]
    shifts = [rs - (r0 + i) + NA_KH - 1 for i, rs in enumerate(key_rows)]
    m_row = 2 * GRID_W

    def scores(p):
        sl = slice(p * LANES, (p + 1) * LANES)
        qs = [_split_heads(q_ref[i * GRID_W:(i + 1) * GRID_W, sl]) for i in range(n_rows)]
        s_win = jnp.concatenate(
            [_qk(qs[i], k_ref[pl.ds(k_offs[i], n_win), sl]) for i in range(n_rows)], axis=0)
        return s_win, _qk(jnp.concatenate(qs, axis=0), kc_ref[:, sl])

    def finish(p, s_win, s_ctx):
        sl = slice(p * LANES, (p + 1) * LANES)
        bias = jnp.concatenate(
            [jnp.concatenate(
                [jnp.concatenate([t2_ref[2 * p, sh + 2 * jj], t2_ref[2 * p + 1, sh + 2 * jj]], axis=0)
                 for jj in range(NA_KH // 2)], axis=1) for sh in shifts], axis=0)
        p_win, p_ctx, l = _softmax_parts(s_win + bias, s_ctx, None)
        o_win = jnp.concatenate(
            [jnp.dot(p_win[i * m_row:(i + 1) * m_row], v_ref[pl.ds(k_offs[i], n_win), sl],
                     preferred_element_type=F32) for i in range(n_rows)], axis=0)
        o = (o_win + jnp.dot(p_ctx, vc_ref[:, sl], preferred_element_type=F32)) / l
        for i in range(n_rows):
            o_ref[i * GRID_W:(i + 1) * GRID_W, sl] = _merge_heads(o[i * m_row:(i + 1) * m_row]).astype(BF16)

    _pipelined(N_HEADS // 2, scores, finish)


def _ctx_body(q_ref, k_ref, v_ref, o_ref):
    for p in range(N_HEADS // 2):
        sl = slice(p * LANES, (p + 1) * LANES)
        qs = _split_heads(q_ref[:, sl])
        s = _qk(qs, k_ref[:, sl])
        m = jnp.max(s, axis=-1, keepdims=True)
        e = jnp.exp2(s - m)
        l = jnp.sum(e, axis=-1, keepdims=True)
        o = jnp.dot(e.astype(BF16), v_ref[:, sl], preferred_element_type=F32)
        o_ref[:, sl] = _merge_heads(o / l).astype(BF16)


def _wa_body(sink_ref, q_ref, k_ref, v_ref, kc_ref, vc_ref, o_ref):
    seq = k_ref.shape[0]
    blk = WA_WINDOW
    n_blk = q_ref.shape[0] // blk
    n_win = 3 * blk
    qb0 = pl.program_id(1) * n_blk
    starts = [pl.multiple_of(jnp.clip((qb0 + b - 1) * blk, 0, seq - n_win), blk) for b in range(n_blk)]
    d_row = lax.broadcasted_iota(jnp.int32, (blk, n_win), 0) - lax.broadcasted_iota(jnp.int32, (blk, n_win), 1)
    bands = []
    for b in range(n_blk):
        dist = d_row + ((qb0 + b) * blk - starts[b])
        band = jnp.where(jnp.abs(dist) <= WA_WINDOW, 0.0, NEG_INF).astype(F32)
        bands += [band, band]
    band = jnp.concatenate(bands, axis=0)
    pairs_per_kv = N_HEADS // WA_KV_HEADS // 2
    m_blk = 2 * blk

    def scores(p):
        kv = slice(p // pairs_per_kv * LANES, (p // pairs_per_kv + 1) * LANES)
        qs = [_split_heads(q_ref[b * blk:(b + 1) * blk, p * LANES:(p + 1) * LANES]) for b in range(n_blk)]
        s_win = jnp.concatenate(
            [_qk(qs[b], k_ref[pl.ds(starts[b], n_win), kv]) for b in range(n_blk)], axis=0)
        return s_win, _qk(jnp.concatenate(qs, axis=0), kc_ref[:, kv])

    def finish(p, s_win, s_ctx):
        kv = slice(p // pairs_per_kv * LANES, (p // pairs_per_kv + 1) * LANES)
        sink = jnp.concatenate(
            [jnp.full((blk, 1), sink_ref[2 * p + j] * LOG2E, F32) for j in range(2)] * n_blk, axis=0)
        p_win, p_ctx, l = _softmax_parts(s_win + band, s_ctx, sink)
        o_win = jnp.concatenate(
            [jnp.dot(p_win[b * m_blk:(b + 1) * m_blk], v_ref[pl.ds(starts[b], n_win), kv],
                     preferred_element_type=F32) for b in range(n_blk)], axis=0)
        o = (o_win + jnp.dot(p_ctx, vc_ref[:, kv], preferred_element_type=F32)) / l
        for b in range(n_blk):
            o_ref[b * blk:(b + 1) * blk, p * LANES:(p + 1) * LANES] = _merge_heads(
                o[b * m_blk:(b + 1) * m_blk]).astype(BF16)

    _pipelined(N_HEADS // 2, scores, finish)


def _na_bias_table(rpb):
    qcol = np.arange(GRID_W)[:, None]
    kcol = np.arange(GRID_W)[None, :]
    win_start = np.clip(qcol - NA_KW // 2, 0, GRID_W - NA_KW)
    valid = (kcol >= win_start) & (kcol < win_start + NA_KW)
    off = np.clip(kcol - qcol, -(NA_KW - 1), NA_KW - 1) + NA_KW - 1
    onehot = (off[None] == np.arange(2 * NA_KW - 1)[:, None, None]).astype(np.float32)
    t = jnp.einsum("hdo,oqk->hdqk", rpb.astype(F32), onehot, precision=lax.Precision.HIGHEST)
    t = jnp.where(valid, t * LOG2E, NEG_INF)
    return jnp.concatenate([t[:, :-1], t[:, 1:]], axis=-1)


def _rope_tables(seq, pad_rows):
    t = jnp.arange(seq, dtype=jnp.int32)
    n = HEAD_DIM // 4
    inv_freq = ROPE_BASE ** (-jnp.arange(n, dtype=F32) / n)
    ang_r = (t // GRID_W).astype(F32)[:, None] * inv_freq[None, :]
    ang_c = (t % GRID_W).astype(F32)[:, None] * inv_freq[None, :]
    ang = jnp.concatenate([ang_r, ang_r, ang_c, ang_c], axis=1)
    sign = np.tile(np.concatenate([-np.ones(n), np.ones(n)]), 2).astype(np.float32)
    cos = jnp.cos(ang)
    sin = jnp.sin(ang) * sign
    cos = jnp.concatenate([cos, jnp.ones((pad_rows, HEAD_DIM), F32)], axis=0)
    sin = jnp.concatenate([sin, jnp.zeros((pad_rows, HEAD_DIM), F32)], axis=0)
    return jnp.tile(cos, (1, LANES // HEAD_DIM)), jnp.tile(sin, (1, LANES // HEAD_DIM))


def _dup_heads(w, n_heads):
    d = w.shape[0]
    w = w.reshape(d, n_heads, 1, HEAD_DIM)
    return jnp.broadcast_to(w, (d, n_heads, LANES // HEAD_DIM, HEAD_DIM)).reshape(d, n_heads * LANES)


def kernel(x, c, ctx, c_ctx, w_mod, b_mod, ln_g, ln_b, ffn_w_in, ffn_w_out,
           na_w_qkv, na_w_o, na_rpb, wa_w_qkv, wa_w_o, wa_sinks):
    bsz, seq, d = x.shape
    n_ctx = ctx.shape[1]
    depth = w_mod.shape[0]
    d_ff = ffn_w_out.shape[2]
    alpha = (2 * depth) ** 0.25
    t_lat, t_ctx = bsz * seq, bsz * n_ctx
    t_all = t_lat + t_ctx
    nt_lat, nt_all = t_lat // TOKEN_TILE, t_all // TOKEN_TILE
    tiles_per_seq = seq // TOKEN_TILE
    dq = N_HEADS * HEAD_DIM
    dkv = WA_KV_HEADS * HEAD_DIM

    mod_rows = 16
    cc = jnp.concatenate([c, c_ctx[None], jnp.zeros((mod_rows - bsz - 1, d), F32)], axis=0)
    mods = _modulation(cc, w_mod, b_mod)

    w_in = ffn_w_in.astype(BF16)
    w_out = ffn_w_out.astype(BF16)
    na_qkv = na_w_qkv.astype(BF16)
    na_o = na_w_o.astype(BF16)
    wa_o = wa_w_o.astype(BF16)
    h_parts = [x.reshape(t_lat, d), ctx.reshape(t_ctx, d)]
    ctx_blk0 = t_lat // n_ctx

    for i in range(depth):
        last = i == depth - 1
        is_na = i % 2 == 0
        j = i // 2
        mod_spec = _mod_spec(i, nt_lat, tiles_per_seq, bsz, d)
        ln_spec = pl.BlockSpec((None, 3, d), lambda t, i=i: (i, 0, 0))

        h_specs = _split_tok_specs(d, nt_lat) if len(h_parts) == 2 else [_tok_spec(d)]
        h = pl.pallas_call(
            functools.partial(_ffn_body, alpha=alpha, n_x=len(h_parts), n_lat_tiles=nt_lat),
            grid=(nt_all,),
            in_specs=[mod_spec, *h_specs, _resident((d, 2 * d_ff), (i, 0)), _resident((d_ff, d), (i, 0)),
                      ln_spec, ln_spec],
            out_specs=_tok_spec(d),
            out_shape=jax.ShapeDtypeStruct((t_all, d), F32),
            compiler_params=_cparams(1),
            name=f"ffn_pre_{i}",
        )(mods, *h_parts, w_in, w_out, ln_g, ln_b)

        if is_na:
            q, k, v = pl.pallas_call(
                _qkv_na_body,
                grid=(nt_all,),
                in_specs=[mod_spec, _tok_spec(d), _resident((d, 3 * dq), (j,))],
                out_specs=[_tok_spec(dq)] * 3,
                out_shape=[jax.ShapeDtypeStruct((t_all, dq), BF16)] * 3,
                compiler_params=_cparams(1),
                name=f"qkv_na_{i}",
            )(mods, h, na_qkv)
            t2 = _na_bias_table(na_rpb[j])
            q_rows = NA_ROWS_PER_STEP * GRID_W
            steps = seq // q_rows
            att = pl.pallas_call(
                _na_body,
                grid=(bsz, steps),
                in_specs=[
                    pl.BlockSpec((q_rows, dq), lambda b, r: (b * steps + r, 0)),
                    pl.BlockSpec((seq, dq), lambda b, r: (b, 0)),
                    pl.BlockSpec((seq, dq), lambda b, r: (b, 0)),
                    pl.BlockSpec((n_ctx, dq), lambda b, r: (ctx_blk0 + b, 0)),
                    pl.BlockSpec((n_ctx, dq), lambda b, r: (ctx_blk0 + b, 0)),
                    _resident(t2.shape),
                ],
                out_specs=pl.BlockSpec((q_rows, dq), lambda b, r: (b * steps + r, 0)),
                out_shape=jax.ShapeDtypeStruct((t_lat, dq), BF16),
                compiler_params=_cparams(2),
                name=f"attn_na_{i}",
            )(q, k, v, k, v, t2)
            w_o, w_o_spec = na_o, _resident((dq, d), (j,))
        else:
            wq = wa_w_qkv[j]
            w_qkv = jnp.concatenate(
                [wq[:, :dq], _dup_heads(wq[:, dq:dq + dkv], WA_KV_HEADS), _dup_heads(wq[:, dq + dkv:], WA_KV_HEADS)],
                axis=1).astype(BF16)
            nkv = WA_KV_HEADS * LANES
            cos, sin = _rope_tables(seq, TOKEN_TILE)
            rope_spec = pl.BlockSpec(
                (TOKEN_TILE, LANES), lambda t: (jnp.where(t < nt_lat, t % tiles_per_seq, tiles_per_seq), 0))
            q, k, v = pl.pallas_call(
                _qkv_wa_body,
                grid=(nt_all,),
                in_specs=[mod_spec, _tok_spec(d), rope_spec, rope_spec, _resident((d, dq + 2 * nkv))],
                out_specs=[_tok_spec(dq), _tok_spec(nkv), _tok_spec(nkv)],
                out_shape=[jax.ShapeDtypeStruct((t_all, dq), BF16),
                           jax.ShapeDtypeStruct((t_all, nkv), BF16),
                           jax.ShapeDtypeStruct((t_all, nkv), BF16)],
                compiler_params=_cparams(1),
                name=f"qkv_wa_{i}",
            )(mods, h, cos, sin, w_qkv)
            blk = WA_BLOCKS_PER_STEP * WA_WINDOW
            steps = seq // blk
            att = pl.pallas_call(
                _wa_body,
                grid=(bsz, steps),
                in_specs=[
                    pl.BlockSpec(memory_space=pltpu.SMEM),
                    pl.BlockSpec((blk, dq), lambda b, r: (b * steps + r, 0)),
                    pl.BlockSpec((seq, nkv), lambda b, r: (b, 0)),
                    pl.BlockSpec((seq, nkv), lambda b, r: (b, 0)),
                    pl.BlockSpec((n_ctx, nkv), lambda b, r: (ctx_blk0 + b, 0)),
                    pl.BlockSpec((n_ctx, nkv), lambda b, r: (ctx_blk0 + b, 0)),
                ],
                out_specs=pl.BlockSpec((blk, dq), lambda b, r: (b * steps + r, 0)),
                out_shape=jax.ShapeDtypeStruct((t_lat, dq), BF16),
                compiler_params=_cparams(2),
                name=f"attn_wa_{i}",
            )(wa_sinks[j], q, k, v, k, v)
            w_o, w_o_spec = wa_o, _resident((dq, d), (j,))

        att_parts = [att]
        if not last:
            ctx_spec = pl.BlockSpec((n_ctx, dq), lambda b: (ctx_blk0 + b, 0))
            att_parts.append(pl.pallas_call(
                _ctx_body,
                grid=(bsz,),
                in_specs=[ctx_spec] * 3,
                out_specs=pl.BlockSpec((n_ctx, dq), lambda b: (b, 0)),
                out_shape=jax.ShapeDtypeStruct((t_ctx, dq), BF16),
                compiler_params=_cparams(1),
                name=f"attn_ctx_{i}",
            )(q, k, v))

        n_tiles = nt_lat if last else nt_all
        att_specs = _split_tok_specs(dq, nt_lat) if len(att_parts) == 2 else [_tok_spec(dq)]
        h = pl.pallas_call(
            functools.partial(_post_body, alpha=alpha, n_a=len(att_parts), n_lat_tiles=nt_lat),
            grid=(n_tiles,),
            in_specs=[mod_spec, _tok_spec(d), *att_specs, w_o_spec,
                      _resident((d, 2 * d_ff), (i, 1)), _resident((d_ff, d), (i, 1)), ln_spec, ln_spec],
            out_specs=_tok_spec(d),
            out_shape=jax.ShapeDtypeStruct((n_tiles * TOKEN_TILE, d), F32),
            compiler_params=_cparams(1),
            name=f"post_{i}",
        )(mods, h, *att_parts, w_o, w_in, w_out, ln_g, ln_b)
        h_parts = [h]

    return h.reshape(bsz, seq, d)
```

```python
import functools

import numpy as np
import jax
import jax.numpy as jnp
from jax import lax
from jax.experimental import pallas as pl
from jax.experimental.pallas import tpu as pltpu

F32 = jnp.float32
BF16 = jnp.bfloat16

GRID_W = 64
N_HEADS = 16
HEAD_DIM = 64
NA_KH = 8
NA_KW = 16
WA_KV_HEADS = 4
WA_WINDOW = 128
FFN_RES = 0.5
ROPE_BASE = 10000.0
N_MOD = 9
LN_EPS = 1e-5
NEG_INF = -1e30

LANES = 128
TOKEN_TILE = 512
FIRST_ROW_CHUNK = 128
ROW_CHUNK = 256
NA_ROWS_PER_STEP = 8
WA_BLOCKS_PER_STEP = 4
ATTN_LOOKAHEAD = 3
LOG2E = 1.4426950408889634
Q_SCALE = HEAD_DIM ** -0.5 * LOG2E
VMEM_LIMIT = 56 * 1024 * 1024


def _cparams(n_grid):
    return pltpu.CompilerParams(
        dimension_semantics=("arbitrary",) * n_grid, vmem_limit_bytes=VMEM_LIMIT)


def _resident(shape, lead=()):
    lead = tuple(lead)
    nd = len(shape)
    return pl.BlockSpec((None,) * len(lead) + tuple(shape), lambda *_: lead + (0,) * nd,
                        pipeline_mode=pl.Buffered(1))


def _tokens(refs, n_lat_tiles, rows):
    if len(refs) == 1:
        return refs[0][rows, :]
    return jnp.where(pl.program_id(0) < n_lat_tiles, refs[0][rows, :], refs[1][rows, :])


def _row_chunks(n):
    bounds = [0] + list(range(FIRST_ROW_CHUNK, n, ROW_CHUNK)) + [n]
    return [slice(a, b) for a, b in zip(bounds[:-1], bounds[1:])]


def _layer_norm(z, g, b):
    mu = jnp.mean(z, axis=-1, keepdims=True)
    var = jnp.mean(z * z, axis=-1, keepdims=True) - mu * mu
    return (z - mu) * lax.rsqrt(var + LN_EPS) * g + b


def _swiglu(u_bf16, win_ref, wout_ref):
    d_ff = wout_ref.shape[0]
    a = jnp.dot(u_bf16, win_ref[:, :d_ff], preferred_element_type=F32)
    v = jnp.dot(u_bf16, win_ref[:, d_ff:], preferred_element_type=F32)
    g = (a * jax.nn.sigmoid(a) * v).astype(BF16)
    return jnp.dot(g, wout_ref[...], preferred_element_type=F32)


def _ffn_residual(x, mod, base, win_ref, wout_ref, alpha):
    shift, scale, gate = mod[base:base + 1], mod[base + 1:base + 2], mod[base + 2:base + 3]
    u = (x * (1.0 + scale) + shift).astype(BF16)
    y = _swiglu(u, win_ref, wout_ref)
    return alpha * x + (FFN_RES * gate) * y


def _mod_body(c_ref, w_ref, b_ref, o_ref):
    c = c_ref[...]
    s = (c * jax.nn.sigmoid(c)).astype(BF16)
    o_ref[0] = jnp.dot(s, w_ref[0].astype(BF16), preferred_element_type=F32) + b_ref[0]


def _modulation(cc, w_mod, b_mod):
    depth, d, nd = w_mod.shape
    r = cc.shape[0]
    out = pl.pallas_call(
        _mod_body,
        grid=(depth, nd // d),
        in_specs=[
            pl.BlockSpec((r, d), lambda i, j: (0, 0)),
            pl.BlockSpec((1, d, d), lambda i, j: (i, 0, j)),
            pl.BlockSpec((1, 1, d), lambda i, j: (i, 0, j)),
        ],
        out_specs=pl.BlockSpec((1, r, d), lambda i, j: (i, 0, j)),
        out_shape=jax.ShapeDtypeStruct((depth, r, nd), F32),
        compiler_params=_cparams(2),
        name="modulation",
    )(cc, w_mod, b_mod.reshape(depth, 1, nd))
    return out.reshape(depth, r, nd // d, d)


def _ffn_body(mod_ref, *refs, alpha, n_x, n_lat_tiles):
    x_refs, (win_ref, wout_ref, g_ref, b_ref, o_ref) = refs[:n_x], refs[n_x:]
    for rows in _row_chunks(o_ref.shape[0]):
        z = _ffn_residual(_tokens(x_refs, n_lat_tiles, rows), mod_ref[0], 0, win_ref, wout_ref, alpha)
        o_ref[rows, :] = _layer_norm(z, g_ref[0:1], b_ref[0:1])


def _post_body(mod_ref, h_ref, *refs, alpha, n_a, n_lat_tiles):
    a_refs, (wo_ref, win_ref, wout_ref, g_ref, b_ref, o_ref) = refs[:n_a], refs[n_a:]

    def mixer_out(rows):
        y = jnp.dot(_tokens(a_refs, n_lat_tiles, rows), wo_ref[...], preferred_element_type=F32)
        return _layer_norm(alpha * h_ref[rows, :] + mod_ref[0][5:6] * y, g_ref[1:2], b_ref[1:2])

    chunks = _row_chunks(o_ref.shape[0])
    nxt = mixer_out(chunks[0])
    for c, rows in enumerate(chunks):
        cur = nxt
        if c + 1 < len(chunks):
            nxt = mixer_out(chunks[c + 1])
        z = _ffn_residual(cur, mod_ref[0], 6, win_ref, wout_ref, alpha)
        o_ref[rows, :] = _layer_norm(z, g_ref[2:3], b_ref[2:3])


def _qkv_na_body(mod_ref, h_ref, w_ref, q_ref, k_ref, v_ref):
    mod = mod_ref[0]
    d = h_ref.shape[1]
    u = (h_ref[...] * (1.0 + mod[4:5]) + mod[3:4]).astype(BF16)
    r = jnp.dot(u, w_ref[...], preferred_element_type=F32)
    q_ref[...] = (r[:, :d] * Q_SCALE).astype(BF16)
    k_ref[...] = r[:, d:2 * d].astype(BF16)
    v_ref[...] = r[:, 2 * d:].astype(BF16)


def _rope_lanes(x, cos, sin_signed, first_half):
    partner = jnp.where(first_half, pltpu.roll(x, LANES - 16, 1), pltpu.roll(x, 16, 1))
    return x * cos + partner * sin_signed


def _qkv_wa_body(mod_ref, h_ref, cos_ref, sin_ref, w_ref, q_ref, k_ref, v_ref):
    mod = mod_ref[0]
    d = h_ref.shape[1]
    nk = k_ref.shape[1]
    u = (h_ref[...] * (1.0 + mod[4:5]) + mod[3:4]).astype(BF16)
    r = jnp.dot(u, w_ref[...], preferred_element_type=F32)
    cos, sin = cos_ref[...], sin_ref[...]
    first_half = lax.broadcasted_iota(jnp.int32, cos.shape, 1) % 32 < 16
    for j in range(d // LANES):
        sl = slice(j * LANES, (j + 1) * LANES)
        q_ref[:, sl] = (_rope_lanes(r[:, sl], cos, sin, first_half) * Q_SCALE).astype(BF16)
    for j in range(nk // LANES):
        sl = slice(d + j * LANES, d + (j + 1) * LANES)
        k_ref[:, j * LANES:(j + 1) * LANES] = _rope_lanes(r[:, sl], cos, sin, first_half).astype(BF16)
    v_ref[...] = r[:, d + nk:].astype(BF16)


def _mod_spec(layer, n_lat_tiles, tiles_per_seq, ctx_row, d):
    return pl.BlockSpec(
        (None, 1, N_MOD, d),
        lambda t: (layer, jnp.where(t < n_lat_tiles, t // tiles_per_seq, ctx_row), 0, 0))


def _tok_spec(width):
    return pl.BlockSpec((TOKEN_TILE, width), lambda t: (t, 0))


def _split_tok_specs(width, n_lat_tiles):
    return [pl.BlockSpec((TOKEN_TILE, width), lambda t: (jnp.minimum(t, n_lat_tiles - 1), 0)),
            pl.BlockSpec((TOKEN_TILE, width), lambda t: (jnp.maximum(t - n_lat_tiles, 0), 0))]


def _split_heads(x):
    lo = lax.broadcasted_iota(jnp.int32, x.shape, 1) < HEAD_DIM
    zero = jnp.zeros_like(x)
    return jnp.concatenate([jnp.where(lo, x, zero), jnp.where(lo, zero, x)], axis=0)


def _merge_heads(o):
    n = o.shape[0] // 2
    lo = lax.broadcasted_iota(jnp.int32, (n, LANES), 1) < HEAD_DIM
    return jnp.where(lo, o[:n], o[n:])


def _qk(q, k):
    return lax.dot_general(q, k, (((1,), (1,)), ((), ())), preferred_element_type=F32)


def _lane_fold(op, *xs):
    parts = [x[:, i:i + LANES] for x in xs for i in range(0, x.shape[1], LANES)]
    return functools.reduce(op, parts)


def _softmax_numerators(s_win, s_ctx, sink):
    x_win, x_ctx = s_win.astype(BF16), s_ctx.astype(BF16)
    m = jnp.max(_lane_fold(jnp.maximum, x_win, x_ctx).astype(F32), axis=-1, keepdims=True)
    if sink is not None:
        m = jnp.maximum(m, sink)
    m_bf16 = m.astype(BF16)
    return jnp.exp2(x_win - m_bf16), jnp.exp2(x_ctx - m_bf16), m_bf16.astype(F32)


def _with_ones(v):
    return jnp.concatenate([v, jnp.ones_like(v)], axis=1)


def _normalise(o_sum, extra):
    l = o_sum[:, LANES:LANES + 1]
    if extra is not None:
        l = l + extra
    return o_sum[:, :LANES] / l


def _pipelined(n, scores, finish):
    pending = [scores(i) for i in range(min(ATTN_LOOKAHEAD, n))]
    for i in range(n):
        if i + ATTN_LOOKAHEAD < n:
            pending.append(scores(i + ATTN_LOOKAHEAD))
        finish(i, *pending.pop(0))


def _na_body(q_ref, k_ref, v_ref, kc_ref, vc_ref, t2_ref, o_ref):
    rows = k_ref.shape[0] // GRID_W
    n_win = NA_KH * GRID_W
    n_rows = q_ref.shape[0] // GRID_W
    r0 = pl.program_id(1) * n_rows
    key_rows = [jnp.clip(r0 + i - NA_KH // 2, 0, rows - NA_KH) for i in range(n_rows)]
    k_offs = [pl.multiple_of(rs * GRID_W, GRID_W) for rs in key_rows]
    shifts = [rs - (r0 + i) + NA_KH - 1 for i, rs in enumerate(key_rows)]
    m_row = 2 * GRID_W

    def scores(p):
        sl = slice(p * LANES, (p + 1) * LANES)
        qs = [_split_heads(q_ref[i * GRID_W:(i + 1) * GRID_W, sl]) for i in range(n_rows)]
        s_win = jnp.concatenate(
            [_qk(qs[i], k_ref[pl.ds(k_offs[i], n_win), sl]) for i in range(n_rows)], axis=0)
        return s_win, _qk(jnp.concatenate(qs, axis=0), kc_ref[:, sl])

    def finish(p, s_win, s_ctx):
        sl = slice(p * LANES, (p + 1) * LANES)
        bias = jnp.concatenate(
            [jnp.concatenate(
                [jnp.concatenate([t2_ref[2 * p, sh + 2 * jj], t2_ref[2 * p + 1, sh + 2 * jj]], axis=0)
                 for jj in range(NA_KH // 2)], axis=1) for sh in shifts], axis=0)
        p_win, p_ctx, _ = _softmax_numerators(s_win + bias, s_ctx, None)
        o_win = jnp.concatenate(
            [jnp.dot(p_win[i * m_row:(i + 1) * m_row], _with_ones(v_ref[pl.ds(k_offs[i], n_win), sl]),
                     preferred_element_type=F32) for i in range(n_rows)], axis=0)
        o = _normalise(o_win + jnp.dot(p_ctx, _with_ones(vc_ref[:, sl]), preferred_element_type=F32), None)
        for i in range(n_rows):
            o_ref[i * GRID_W:(i + 1) * GRID_W, sl] = _merge_heads(o[i * m_row:(i + 1) * m_row]).astype(BF16)

    _pipelined(N_HEADS // 2, scores, finish)


def _ctx_body(q_ref, k_ref, v_ref, o_ref):
    for p in range(N_HEADS // 2):
        sl = slice(p * LANES, (p + 1) * LANES)
        qs = _split_heads(q_ref[:, sl])
        s = _qk(qs, k_ref[:, sl])
        m = jnp.max(s, axis=-1, keepdims=True)
        e = jnp.exp2(s - m)
        l = jnp.sum(e, axis=-1, keepdims=True)
        o = jnp.dot(e.astype(BF16), v_ref[:, sl], preferred_element_type=F32)
        o_ref[:, sl] = _merge_heads(o / l).astype(BF16)


def _wa_body(sink_ref, q_ref, k_ref, v_ref, kc_ref, vc_ref, o_ref):
    seq = k_ref.shape[0]
    blk = WA_WINDOW
    n_blk = q_ref.shape[0] // blk
    n_win = 3 * blk
    qb0 = pl.program_id(1) * n_blk
    starts = [pl.multiple_of(jnp.clip((qb0 + b - 1) * blk, 0, seq - n_win), blk) for b in range(n_blk)]
    d_row = lax.broadcasted_iota(jnp.int32, (blk, n_win), 0) - lax.broadcasted_iota(jnp.int32, (blk, n_win), 1)
    bands = []
    for b in range(n_blk):
        dist = d_row + ((qb0 + b) * blk - starts[b])
        band = jnp.where(jnp.abs(dist) <= WA_WINDOW, 0.0, NEG_INF).astype(F32)
        bands += [band, band]
    band = jnp.concatenate(bands, axis=0)
    pairs_per_kv = N_HEADS // WA_KV_HEADS // 2
    m_blk = 2 * blk

    def scores(p):
        kv = slice(p // pairs_per_kv * LANES, (p // pairs_per_kv + 1) * LANES)
        qs = [_split_heads(q_ref[b * blk:(b + 1) * blk, p * LANES:(p + 1) * LANES]) for b in range(n_blk)]
        s_win = jnp.concatenate(
            [_qk(qs[b], k_ref[pl.ds(starts[b], n_win), kv]) for b in range(n_blk)], axis=0)
        return s_win, _qk(jnp.concatenate(qs, axis=0), kc_ref[:, kv])

    def finish(p, s_win, s_ctx):
        kv = slice(p // pairs_per_kv * LANES, (p // pairs_per_kv + 1) * LANES)
        sink = jnp.concatenate(
            [jnp.full((blk, 1), sink_ref[2 * p + j] * LOG2E, F32) for j in range(2)] * n_blk, axis=0)
        p_win, p_ctx, m = _softmax_numerators(s_win + band, s_ctx, sink)
        o_win = jnp.concatenate(
            [jnp.dot(p_win[b * m_blk:(b + 1) * m_blk], _with_ones(v_ref[pl.ds(starts[b], n_win), kv]),
                     preferred_element_type=F32) for b in range(n_blk)], axis=0)
        o = _normalise(o_win + jnp.dot(p_ctx, _with_ones(vc_ref[:, kv]), preferred_element_type=F32),
                       jnp.exp2(sink - m))
        for b in range(n_blk):
            o_ref[b * blk:(b + 1) * blk, p * LANES:(p + 1) * LANES] = _merge_heads(
                o[b * m_blk:(b + 1) * m_blk]).astype(BF16)

    _pipelined(N_HEADS // 2, scores, finish)


def _attention_specs(q_rows, dq, nkv, bsz, seq, n_ctx):
    steps = seq // q_rows
    ctx_blk0 = bsz * seq // n_ctx
    q_spec = pl.BlockSpec((q_rows, dq), lambda b, r: (b * steps + r, 0))
    kv_spec = pl.BlockSpec((seq, nkv), lambda b, r: (b, 0))
    ctx_spec = pl.BlockSpec((n_ctx, nkv), lambda b, r: (ctx_blk0 + b, 0))
    return (bsz, steps), q_spec, [kv_spec, kv_spec, ctx_spec, ctx_spec]


def _na_attention(q, k, v, t2, bsz, seq, n_ctx, name):
    dq = q.shape[1]
    grid, q_spec, kv_specs = _attention_specs(NA_ROWS_PER_STEP * GRID_W, dq, dq, bsz, seq, n_ctx)
    return pl.pallas_call(
        _na_body,
        grid=grid,
        in_specs=[q_spec, *kv_specs, _resident(t2.shape)],
        out_specs=q_spec,
        out_shape=jax.ShapeDtypeStruct((bsz * seq, dq), BF16),
        compiler_params=_cparams(2),
        name=name,
    )(q, k, v, k, v, t2)


def _wa_attention(sinks, q, k, v, bsz, seq, n_ctx, name):
    dq = q.shape[1]
    grid, q_spec, kv_specs = _attention_specs(WA_BLOCKS_PER_STEP * WA_WINDOW, dq, k.shape[1], bsz, seq, n_ctx)
    return pl.pallas_call(
        _wa_body,
        grid=grid,
        in_specs=[pl.BlockSpec(memory_space=pltpu.SMEM), q_spec, *kv_specs],
        out_specs=q_spec,
        out_shape=jax.ShapeDtypeStruct((bsz * seq, dq), BF16),
        compiler_params=_cparams(2),
        name=name,
    )(sinks, q, k, v, k, v)


def _na_bias_table(rpb):
    qcol = np.arange(GRID_W)[:, None]
    kcol = np.arange(GRID_W)[None, :]
    win_start = np.clip(qcol - NA_KW // 2, 0, GRID_W - NA_KW)
    valid = (kcol >= win_start) & (kcol < win_start + NA_KW)
    off = np.clip(kcol - qcol, -(NA_KW - 1), NA_KW - 1) + NA_KW - 1
    onehot = (off[None] == np.arange(2 * NA_KW - 1)[:, None, None]).astype(np.float32)
    t = jnp.einsum("hdo,oqk->hdqk", rpb.astype(F32), onehot, precision=lax.Precision.HIGHEST)
    t = jnp.where(valid, t * LOG2E, NEG_INF)
    return jnp.concatenate([t[:, :-1], t[:, 1:]], axis=-1)


def _rope_tables(seq, pad_rows):
    t = jnp.arange(seq, dtype=jnp.int32)
    n = HEAD_DIM // 4
    inv_freq = ROPE_BASE ** (-jnp.arange(n, dtype=F32) / n)
    ang_r = (t // GRID_W).astype(F32)[:, None] * inv_freq[None, :]
    ang_c = (t % GRID_W).astype(F32)[:, None] * inv_freq[None, :]
    ang = jnp.concatenate([ang_r, ang_r, ang_c, ang_c], axis=1)
    sign = np.tile(np.concatenate([-np.ones(n), np.ones(n)]), 2).astype(np.float32)
    cos = jnp.cos(ang)
    sin = jnp.sin(ang) * sign
    cos = jnp.concatenate([cos, jnp.ones((pad_rows, HEAD_DIM), F32)], axis=0)
    sin = jnp.concatenate([sin, jnp.zeros((pad_rows, HEAD_DIM), F32)], axis=0)
    return jnp.tile(cos, (1, LANES // HEAD_DIM)), jnp.tile(sin, (1, LANES // HEAD_DIM))


def _dup_heads(w, n_heads):
    d = w.shape[0]
    w = w.reshape(d, n_heads, 1, HEAD_DIM)
    return jnp.broadcast_to(w, (d, n_heads, LANES // HEAD_DIM, HEAD_DIM)).reshape(d, n_heads * LANES)


def kernel(x, c, ctx, c_ctx, w_mod, b_mod, ln_g, ln_b, ffn_w_in, ffn_w_out,
           na_w_qkv, na_w_o, na_rpb, wa_w_qkv, wa_w_o, wa_sinks):
    bsz, seq, d = x.shape
    n_ctx = ctx.shape[1]
    depth = w_mod.shape[0]
    d_ff = ffn_w_out.shape[2]
    alpha = (2 * depth) ** 0.25
    t_lat, t_ctx = bsz * seq, bsz * n_ctx
    t_all = t_lat + t_ctx
    nt_lat, nt_all = t_lat // TOKEN_TILE, t_all // TOKEN_TILE
    tiles_per_seq = seq // TOKEN_TILE
    dq = N_HEADS * HEAD_DIM
    dkv = WA_KV_HEADS * HEAD_DIM

    mod_rows = 16
    cc = jnp.concatenate([c, c_ctx[None], jnp.zeros((mod_rows - bsz - 1, d), F32)], axis=0)
    mods = _modulation(cc, w_mod, b_mod)

    w_in = ffn_w_in.astype(BF16)
    w_out = ffn_w_out.astype(BF16)
    na_qkv = na_w_qkv.astype(BF16)
    na_o = na_w_o.astype(BF16)
    wa_o = wa_w_o.astype(BF16)
    h_parts = [x.reshape(t_lat, d), ctx.reshape(t_ctx, d)]
    ctx_blk0 = t_lat // n_ctx

    for i in range(depth):
        last = i == depth - 1
        is_na = i % 2 == 0
        j = i // 2
        mod_spec = _mod_spec(i, nt_lat, tiles_per_seq, bsz, d)
        ln_spec = pl.BlockSpec((None, 3, d), lambda t, i=i: (i, 0, 0))

        h_specs = _split_tok_specs(d, nt_lat) if len(h_parts) == 2 else [_tok_spec(d)]
        h = pl.pallas_call(
            functools.partial(_ffn_body, alpha=alpha, n_x=len(h_parts), n_lat_tiles=nt_lat),
            grid=(nt_all,),
            in_specs=[mod_spec, *h_specs, _resident((d, 2 * d_ff), (i, 0)), _resident((d_ff, d), (i, 0)),
                      ln_spec, ln_spec],
            out_specs=_tok_spec(d),
            out_shape=jax.ShapeDtypeStruct((t_all, d), F32),
            compiler_params=_cparams(1),
            name=f"ffn_pre_{i}",
        )(mods, *h_parts, w_in, w_out, ln_g, ln_b)

        if is_na:
            q, k, v = pl.pallas_call(
                _qkv_na_body,
                grid=(nt_all,),
                in_specs=[mod_spec, _tok_spec(d), _resident((d, 3 * dq), (j,))],
                out_specs=[_tok_spec(dq)] * 3,
                out_shape=[jax.ShapeDtypeStruct((t_all, dq), BF16)] * 3,
                compiler_params=_cparams(1),
                name=f"qkv_na_{i}",
            )(mods, h, na_qkv)
            att = _na_attention(q, k, v, _na_bias_table(na_rpb[j]), bsz, seq, n_ctx, f"attn_na_{i}")
            w_o, w_o_spec = na_o, _resident((dq, d), (j,))
        else:
            wq = wa_w_qkv[j]
            w_qkv = jnp.concatenate(
                [wq[:, :dq], _dup_heads(wq[:, dq:dq + dkv], WA_KV_HEADS), _dup_heads(wq[:, dq + dkv:], WA_KV_HEADS)],
                axis=1).astype(BF16)
            nkv = WA_KV_HEADS * LANES
            cos, sin = _rope_tables(seq, TOKEN_TILE)
            rope_spec = pl.BlockSpec(
                (TOKEN_TILE, LANES), lambda t: (jnp.where(t < nt_lat, t % tiles_per_seq, tiles_per_seq), 0))
            q, k, v = pl.pallas_call(
                _qkv_wa_body,
                grid=(nt_all,),
                in_specs=[mod_spec, _tok_spec(d), rope_spec, rope_spec, _resident((d, dq + 2 * nkv))],
                out_specs=[_tok_spec(dq), _tok_spec(nkv), _tok_spec(nkv)],
                out_shape=[jax.ShapeDtypeStruct((t_all, dq), BF16),
                           jax.ShapeDtypeStruct((t_all, nkv), BF16),
                           jax.ShapeDtypeStruct((t_all, nkv), BF16)],
                compiler_params=_cparams(1),
                name=f"qkv_wa_{i}",
            )(mods, h, cos, sin, w_qkv)
            att = _wa_attention(wa_sinks[j], q, k, v, bsz, seq, n_ctx, f"attn_wa_{i}")
            w_o, w_o_spec = wa_o, _resident((dq, d), (j,))

        att_parts = [att]
        if not last:
            ctx_spec = pl.BlockSpec((n_ctx, dq), lambda b: (ctx_blk0 + b, 0))
            att_parts.append(pl.pallas_call(
                _ctx_body,
                grid=(bsz,),
                in_specs=[ctx_spec] * 3,
                out_specs=pl.BlockSpec((n_ctx, dq), lambda b: (b, 0)),
                out_shape=jax.ShapeDtypeStruct((t_ctx, dq), BF16),
                compiler_params=_cparams(1),
                name=f"attn_ctx_{i}",
            )(q, k, v))

        n_tiles = nt_lat if last else nt_all
        att_specs = _split_tok_specs(dq, nt_lat) if len(att_parts) == 2 else [_tok_spec(dq)]
        h = pl.pallas_call(
            functools.partial(_post_body, alpha=alpha, n_a=len(att_parts), n_lat_tiles=nt_lat),
            grid=(n_tiles,),
            in_specs=[mod_spec, _tok_spec(d), *att_specs, w_o_spec,
                      _resident((d, 2 * d_ff), (i, 1)), _resident((d_ff, d), (i, 1)), ln_spec, ln_spec],
            out_specs=_tok_spec(d),
            out_shape=jax.ShapeDtypeStruct((n_tiles * TOKEN_TILE, d), F32),
            compiler_params=_cparams(1),
            name=f"post_{i}",
        )(mods, h, *att_parts, w_o, w_in, w_out, ln_g, ln_b)
        h_parts = [h]

    return h.reshape(bsz, seq, d)
```

```python
import functools

import numpy as np
import jax
import jax.numpy as jnp
from jax import lax
from jax.experimental import pallas as pl
from jax.experimental.pallas import tpu as pltpu

F32 = jnp.float32
BF16 = jnp.bfloat16

GRID_W = 64
N_HEADS = 16
HEAD_DIM = 64
NA_KH = 8
NA_KW = 16
WA_KV_HEADS = 4
WA_WINDOW = 128
FFN_RES = 0.5
ROPE_BASE = 10000.0
N_MOD = 9
LN_EPS = 1e-5
NEG_INF = -1e30

LANES = 128
TOKEN_TILE = 512
MOD_COLS_PER_STEP = 3
CAST_ROWS = 32
FIRST_ROW_CHUNK = 128
ROW_CHUNK = 256
NA_ROWS_PER_STEP = 8
WA_BLOCKS_PER_STEP = 4
ATTN_LOOKAHEAD = 3
LOG2E = 1.4426950408889634
Q_SCALE = HEAD_DIM ** -0.5 * LOG2E
VMEM_LIMIT = 56 * 1024 * 1024


def _cparams(n_grid):
    return pltpu.CompilerParams(
        dimension_semantics=("arbitrary",) * n_grid, vmem_limit_bytes=VMEM_LIMIT)


def _resident(shape, lead=()):
    lead = tuple(lead)
    nd = len(shape)
    return pl.BlockSpec((None,) * len(lead) + tuple(shape), lambda *_: lead + (0,) * nd,
                        pipeline_mode=pl.Buffered(1))


def _tokens(refs, n_lat_tiles, rows):
    if len(refs) == 1:
        return refs[0][rows, :]
    return jnp.where(pl.program_id(0) < n_lat_tiles, refs[0][rows, :], refs[1][rows, :])


def _row_chunks(n):
    bounds = [0] + list(range(FIRST_ROW_CHUNK, n, ROW_CHUNK)) + [n]
    return [slice(a, b) for a, b in zip(bounds[:-1], bounds[1:])]


def _layer_norm(z, g, b):
    mu = jnp.mean(z, axis=-1, keepdims=True)
    var = jnp.mean(z * z, axis=-1, keepdims=True) - mu * mu
    return (z - mu) * lax.rsqrt(var + LN_EPS) * g + b


def _swiglu(u_bf16, win_ref, wout_ref):
    d_ff = wout_ref.shape[0]
    a = jnp.dot(u_bf16, win_ref[:, :d_ff], preferred_element_type=F32)
    v = jnp.dot(u_bf16, win_ref[:, d_ff:], preferred_element_type=F32)
    g = (a * jax.nn.sigmoid(a) * v).astype(BF16)
    return jnp.dot(g, wout_ref[...], preferred_element_type=F32)


def _ffn_residual(x, mod, base, win_ref, wout_ref, alpha):
    shift, scale, gate = mod[base:base + 1], mod[base + 1:base + 2], mod[base + 2:base + 3]
    u = (x * (1.0 + scale) + shift).astype(BF16)
    y = _swiglu(u, win_ref, wout_ref)
    return alpha * x + (FFN_RES * gate) * y


def _mod_body(c_ref, w_ref, b_ref, o_ref):
    c = c_ref[...]
    s = (c * jax.nn.sigmoid(c)).astype(BF16)
    o_ref[0] = jnp.dot(s, w_ref[0].astype(BF16), preferred_element_type=F32) + b_ref[0]


def _modulation(cc, w_mod, b_mod):
    depth, d, nd = w_mod.shape
    r = cc.shape[0]
    cols = MOD_COLS_PER_STEP * d
    out = pl.pallas_call(
        _mod_body,
        grid=(depth, nd // cols),
        in_specs=[
            pl.BlockSpec((r, d), lambda i, j: (0, 0)),
            pl.BlockSpec((1, d, cols), lambda i, j: (i, 0, j)),
            pl.BlockSpec((1, 1, cols), lambda i, j: (i, 0, j)),
        ],
        out_specs=pl.BlockSpec((1, r, cols), lambda i, j: (i, 0, j)),
        out_shape=jax.ShapeDtypeStruct((depth, r, nd), F32),
        compiler_params=_cparams(2),
        name="modulation",
    )(cc, w_mod, b_mod.reshape(depth, 1, nd))
    return out.reshape(depth, r, nd // d, d)


def _split_cast_refs(refs, cast_next):
    if not cast_next:
        return refs, None
    return refs[:-5] + (refs[-3],), (refs[-5], refs[-4], refs[-2], refs[-1])


def _cast_next_weights(cast_refs):
    if cast_refs is not None:
        src_in, src_out, dst_in, dst_out = cast_refs
        dst_in[...] = src_in[...].astype(BF16)
        dst_out[...] = src_out[...].astype(BF16)


def _ffn_body(mod_ref, *refs, alpha, n_x, n_lat_tiles, cast_next):
    refs, cast_refs = _split_cast_refs(refs, cast_next)
    x_refs, (win_ref, wout_ref, g_ref, b_ref, o_ref) = refs[:n_x], refs[n_x:]
    _cast_next_weights(cast_refs)
    for rows in _row_chunks(o_ref.shape[0]):
        z = _ffn_residual(_tokens(x_refs, n_lat_tiles, rows), mod_ref[0], 0, win_ref, wout_ref, alpha)
        o_ref[rows, :] = _layer_norm(z, g_ref[0:1], b_ref[0:1])


def _post_body(mod_ref, h_ref, *refs, alpha, n_a, n_lat_tiles, cast_next):
    refs, cast_refs = _split_cast_refs(refs, cast_next)
    a_refs, (wo_ref, win_ref, wout_ref, g_ref, b_ref, o_ref) = refs[:n_a], refs[n_a:]
    _cast_next_weights(cast_refs)

    def mixer_out(rows):
        y = jnp.dot(_tokens(a_refs, n_lat_tiles, rows), wo_ref[...], preferred_element_type=F32)
        return _layer_norm(alpha * h_ref[rows, :] + mod_ref[0][5:6] * y, g_ref[1:2], b_ref[1:2])

    chunks = _row_chunks(o_ref.shape[0])
    nxt = mixer_out(chunks[0])
    for c, rows in enumerate(chunks):
        cur = nxt
        if c + 1 < len(chunks):
            nxt = mixer_out(chunks[c + 1])
        z = _ffn_residual(cur, mod_ref[0], 6, win_ref, wout_ref, alpha)
        o_ref[rows, :] = _layer_norm(z, g_ref[2:3], b_ref[2:3])


def _qkv_na_body(mod_ref, h_ref, w_ref, q_ref, k_ref, v_ref):
    mod = mod_ref[0]
    d = h_ref.shape[1]
    u = (h_ref[...] * (1.0 + mod[4:5]) + mod[3:4]).astype(BF16)
    r = jnp.dot(u, w_ref[...], preferred_element_type=F32)
    q_ref[...] = (r[:, :d] * Q_SCALE).astype(BF16)
    k_ref[...] = r[:, d:2 * d].astype(BF16)
    v_ref[...] = r[:, 2 * d:].astype(BF16)


def _rope_lanes(x, cos, sin_signed, first_half):
    partner = jnp.where(first_half, pltpu.roll(x, LANES - 16, 1), pltpu.roll(x, 16, 1))
    return x * cos + partner * sin_signed


def _qkv_wa_body(mod_ref, h_ref, cos_ref, sin_ref, w_ref, q_ref, k_ref, v_ref):
    mod = mod_ref[0]
    d = h_ref.shape[1]
    nk = k_ref.shape[1]
    u = (h_ref[...] * (1.0 + mod[4:5]) + mod[3:4]).astype(BF16)
    r = jnp.dot(u, w_ref[...], preferred_element_type=F32)
    cos, sin = cos_ref[...], sin_ref[...]
    first_half = lax.broadcasted_iota(jnp.int32, cos.shape, 1) % 32 < 16
    for j in range(d // LANES):
        sl = slice(j * LANES, (j + 1) * LANES)
        q_ref[:, sl] = (_rope_lanes(r[:, sl], cos, sin, first_half) * Q_SCALE).astype(BF16)
    lo = lax.broadcasted_iota(jnp.int32, cos.shape, 1) < HEAD_DIM
    n_src = nk // (2 * LANES)
    for dst_ref, base, rope in ((k_ref, d, True), (v_ref, d + n_src * LANES, False)):
        for j in range(n_src):
            x = r[:, base + j * LANES:base + (j + 1) * LANES]
            if rope:
                x = _rope_lanes(x, cos, sin, first_half)
            swapped = pltpu.roll(x, HEAD_DIM, 1)
            dst_ref[:, 2 * j * LANES:(2 * j + 1) * LANES] = jnp.where(lo, x, swapped).astype(BF16)
            dst_ref[:, (2 * j + 1) * LANES:(2 * j + 2) * LANES] = jnp.where(lo, swapped, x).astype(BF16)


def _mod_spec(layer, n_lat_tiles, tiles_per_seq, ctx_row, d):
    return pl.BlockSpec(
        (None, 1, N_MOD, d),
        lambda t: (layer, jnp.where(t < n_lat_tiles, t // tiles_per_seq, ctx_row), 0, 0))


def _tok_spec(width):
    return pl.BlockSpec((TOKEN_TILE, width), lambda t: (t, 0))


def _cast_next_specs(layer, half, d, d_ff, n_steps):
    n_in, n_out = d // CAST_ROWS, d // LANES
    assert n_steps >= max(n_in, n_out)
    in_specs = [
        pl.BlockSpec((None, None, CAST_ROWS, 2 * d_ff), lambda t: (layer, half, jnp.minimum(t, n_in - 1), 0)),
        pl.BlockSpec((None, None, d_ff, LANES), lambda t: (layer, half, 0, jnp.minimum(t, n_out - 1)))]
    out_specs = [
        pl.BlockSpec((CAST_ROWS, 2 * d_ff), lambda t: (jnp.minimum(t, n_in - 1), 0)),
        pl.BlockSpec((d_ff, LANES), lambda t: (0, jnp.minimum(t, n_out - 1)))]
    out_shape = [jax.ShapeDtypeStruct((d, 2 * d_ff), BF16), jax.ShapeDtypeStruct((d_ff, d), BF16)]
    return in_specs, out_specs, out_shape


def _split_tok_specs(width, n_lat_tiles):
    return [pl.BlockSpec((TOKEN_TILE, width), lambda t: (jnp.minimum(t, n_lat_tiles - 1), 0)),
            pl.BlockSpec((TOKEN_TILE, width), lambda t: (jnp.maximum(t - n_lat_tiles, 0), 0))]


def _split_heads(x):
    lo = lax.broadcasted_iota(jnp.int32, x.shape, 1) < HEAD_DIM
    zero = jnp.zeros_like(x)
    return jnp.concatenate([jnp.where(lo, x, zero), jnp.where(lo, zero, x)], axis=0)


def _merge_heads(o):
    n = o.shape[0] // 2
    lo = lax.broadcasted_iota(jnp.int32, (n, LANES), 1) < HEAD_DIM
    return jnp.where(lo, o[:n], o[n:])


def _qk(q, k):
    return lax.dot_general(q, k, (((1,), (1,)), ((), ())), preferred_element_type=F32)


def _lane_fold(op, *xs):
    parts = [x[:, i:i + LANES] for x in xs for i in range(0, x.shape[1], LANES)]
    return functools.reduce(op, parts)


def _softmax_numerators(s_win, s_ctx, sink):
    x_win, x_ctx = s_win.astype(BF16), s_ctx.astype(BF16)
    m = jnp.max(_lane_fold(jnp.maximum, x_win, x_ctx).astype(F32), axis=-1, keepdims=True)
    if sink is not None:
        m = jnp.maximum(m, sink)
    m_bf16 = m.astype(BF16)
    return jnp.exp2(x_win - m_bf16), jnp.exp2(x_ctx - m_bf16), m_bf16.astype(F32)


def _with_ones(v):
    return jnp.concatenate([v, jnp.ones_like(v)], axis=1)


def _normalise(o_sum, extra):
    l = o_sum[:, LANES:LANES + 1]
    if extra is not None:
        l = l + extra
    return o_sum[:, :LANES] / l


def _pipelined(n, scores, finish):
    pending = [scores(i) for i in range(min(ATTN_LOOKAHEAD, n))]
    for i in range(n):
        if i + ATTN_LOOKAHEAD < n:
            pending.append(scores(i + ATTN_LOOKAHEAD))
        finish(i, *pending.pop(0))


def _na_body(q_ref, k_ref, v_ref, kc_ref, vc_ref, t2_ref, o_ref):
    rows = k_ref.shape[0] // GRID_W
    n_win = NA_KH * GRID_W
    n_rows = q_ref.shape[0] // GRID_W
    r0 = pl.program_id(1) * n_rows
    key_rows = [jnp.clip(r0 + i - NA_KH // 2, 0, rows - NA_KH) for i in range(n_rows)]
    k_offs = [pl.multiple_of(rs * GRID_W, GRID_W) for rs in key_rows]
    shifts = [rs - (r0 + i) + NA_KH - 1 for i, rs in enumerate(key_rows)]
    m_row = 2 * GRID_W

    def scores(p):
        sl = slice(p * LANES, (p + 1) * LANES)
        qs = [_split_heads(q_ref[i * GRID_W:(i + 1) * GRID_W, sl]) for i in range(n_rows)]
        s_win = jnp.concatenate(
            [_qk(qs[i], k_ref[pl.ds(k_offs[i], n_win), sl]) for i in range(n_rows)], axis=0)
        return s_win, _qk(jnp.concatenate(qs, axis=0), kc_ref[:, sl])

    def finish(p, s_win, s_ctx):
        sl = slice(p * LANES, (p + 1) * LANES)
        bias = jnp.concatenate(
            [jnp.concatenate(
                [jnp.concatenate([t2_ref[2 * p, sh + 2 * jj], t2_ref[2 * p + 1, sh + 2 * jj]], axis=0)
                 for jj in range(NA_KH // 2)], axis=1) for sh in shifts], axis=0)
        p_win, p_ctx, _ = _softmax_numerators(s_win + bias, s_ctx, None)
        o_win = jnp.concatenate(
            [jnp.dot(p_win[i * m_row:(i + 1) * m_row], _with_ones(v_ref[pl.ds(k_offs[i], n_win), sl]),
                     preferred_element_type=F32) for i in range(n_rows)], axis=0)
        o = _normalise(o_win + jnp.dot(p_ctx, _with_ones(vc_ref[:, sl]), preferred_element_type=F32), None)
        for i in range(n_rows):
            o_ref[i * GRID_W:(i + 1) * GRID_W, sl] = _merge_heads(o[i * m_row:(i + 1) * m_row]).astype(BF16)

    _pipelined(N_HEADS // 2, scores, finish)


def _ctx_body(q_ref, k_ref, v_ref, o_ref):
    for p in range(N_HEADS // 2):
        sl = slice(p * LANES, (p + 1) * LANES)
        qs = _split_heads(q_ref[:, sl])
        s = _qk(qs, k_ref[:, sl])
        m = jnp.max(s, axis=-1, keepdims=True)
        e = jnp.exp2(s - m)
        l = jnp.sum(e, axis=-1, keepdims=True)
        o = jnp.dot(e.astype(BF16), v_ref[:, sl], preferred_element_type=F32)
        o_ref[:, sl] = _merge_heads(o / l).astype(BF16)


def _wa_body(sink_ref, q_ref, k_ref, v_ref, kc_ref, vc_ref, o_ref):
    seq = k_ref.shape[0]
    blk = WA_WINDOW
    n_blk = q_ref.shape[0] // blk
    n_win = 3 * blk
    qb0 = pl.program_id(1) * n_blk
    starts = [pl.multiple_of(jnp.clip((qb0 + b - 1) * blk, 0, seq - n_win), blk) for b in range(n_blk)]
    d_row = lax.broadcasted_iota(jnp.int32, (blk, n_win), 0) - lax.broadcasted_iota(jnp.int32, (blk, n_win), 1)
    bands = []
    for b in range(n_blk):
        dist = d_row + ((qb0 + b) * blk - starts[b])
        band = jnp.where(jnp.abs(dist) <= WA_WINDOW, 0.0, NEG_INF).astype(F32)
        bands += [band, band]
    band = jnp.concatenate(bands, axis=0)
    pairs_per_kv = N_HEADS // WA_KV_HEADS // 2
    m_blk = 2 * blk

    def scores(p):
        kv = slice(p // pairs_per_kv * LANES, (p // pairs_per_kv + 1) * LANES)
        qs = [_split_heads(q_ref[b * blk:(b + 1) * blk, p * LANES:(p + 1) * LANES]) for b in range(n_blk)]
        s_win = jnp.concatenate(
            [_qk(qs[b], k_ref[pl.ds(starts[b], n_win), kv]) for b in range(n_blk)], axis=0)
        return s_win, _qk(jnp.concatenate(qs, axis=0), kc_ref[:, kv])

    def finish(p, s_win, s_ctx):
        kv = slice(p // pairs_per_kv * LANES, (p // pairs_per_kv + 1) * LANES)
        sink = jnp.concatenate(
            [jnp.full((blk, 1), sink_ref[2 * p + j] * LOG2E, F32) for j in range(2)] * n_blk, axis=0)
        p_win, p_ctx, m = _softmax_numerators(s_win + band, s_ctx, sink)
        o_win = jnp.concatenate(
            [jnp.dot(p_win[b * m_blk:(b + 1) * m_blk], _with_ones(v_ref[pl.ds(starts[b], n_win), kv]),
                     preferred_element_type=F32) for b in range(n_blk)], axis=0)
        o = _normalise(o_win + jnp.dot(p_ctx, _with_ones(vc_ref[:, kv]), preferred_element_type=F32),
                       jnp.exp2(sink - m))
        for b in range(n_blk):
            o_ref[b * blk:(b + 1) * blk, p * LANES:(p + 1) * LANES] = _merge_heads(
                o[b * m_blk:(b + 1) * m_blk]).astype(BF16)

    _pipelined(N_HEADS // 2, scores, finish)


def _attention_specs(q_rows, dq, nkv, bsz, seq, n_ctx):
    steps = seq // q_rows
    ctx_blk0 = bsz * seq // n_ctx
    q_spec = pl.BlockSpec((q_rows, dq), lambda b, r: (b * steps + r, 0))
    kv_spec = pl.BlockSpec((seq, nkv), lambda b, r: (b, 0))
    ctx_spec = pl.BlockSpec((n_ctx, nkv), lambda b, r: (ctx_blk0 + b, 0))
    return (bsz, steps), q_spec, [kv_spec, kv_spec, ctx_spec, ctx_spec]


def _na_attention(q, k, v, t2, bsz, seq, n_ctx, name):
    dq = q.shape[1]
    grid, q_spec, kv_specs = _attention_specs(NA_ROWS_PER_STEP * GRID_W, dq, dq, bsz, seq, n_ctx)
    return pl.pallas_call(
        _na_body,
        grid=grid,
        in_specs=[q_spec, *kv_specs, _resident(t2.shape)],
        out_specs=q_spec,
        out_shape=jax.ShapeDtypeStruct((bsz * seq, dq), BF16),
        compiler_params=_cparams(2),
        name=name,
    )(q, k, v, k, v, t2)


def _wa_attention(sinks, q, k, v, bsz, seq, n_ctx, name):
    dq = q.shape[1]
    grid, q_spec, kv_specs = _attention_specs(WA_BLOCKS_PER_STEP * WA_WINDOW, dq, k.shape[1], bsz, seq, n_ctx)
    return pl.pallas_call(
        _wa_body,
        grid=grid,
        in_specs=[pl.BlockSpec(memory_space=pltpu.SMEM), q_spec, *kv_specs],
        out_specs=q_spec,
        out_shape=jax.ShapeDtypeStruct((bsz * seq, dq), BF16),
        compiler_params=_cparams(2),
        name=name,
    )(sinks, q, k, v, k, v)


def _na_bias_table(rpb):
    qcol = np.arange(GRID_W)[:, None]
    kcol = np.arange(GRID_W)[None, :]
    win_start = np.clip(qcol - NA_KW // 2, 0, GRID_W - NA_KW)
    valid = (kcol >= win_start) & (kcol < win_start + NA_KW)
    off = np.clip(kcol - qcol, -(NA_KW - 1), NA_KW - 1) + NA_KW - 1
    onehot = (off[None] == np.arange(2 * NA_KW - 1)[:, None, None]).astype(np.float32)
    t = jnp.einsum("hdo,oqk->hdqk", rpb.astype(F32), onehot, precision=lax.Precision.HIGHEST)
    t = jnp.where(valid, t * LOG2E, NEG_INF)
    return jnp.concatenate([t[:, :-1], t[:, 1:]], axis=-1)


def _rope_tables(seq, pad_rows):
    t = jnp.arange(seq, dtype=jnp.int32)
    n = HEAD_DIM // 4
    inv_freq = ROPE_BASE ** (-jnp.arange(n, dtype=F32) / n)
    ang_r = (t // GRID_W).astype(F32)[:, None] * inv_freq[None, :]
    ang_c = (t % GRID_W).astype(F32)[:, None] * inv_freq[None, :]
    ang = jnp.concatenate([ang_r, ang_r, ang_c, ang_c], axis=1)
    sign = np.tile(np.concatenate([-np.ones(n), np.ones(n)]), 2).astype(np.float32)
    cos = jnp.cos(ang)
    sin = jnp.sin(ang) * sign
    cos = jnp.concatenate([cos, jnp.ones((pad_rows, HEAD_DIM), F32)], axis=0)
    sin = jnp.concatenate([sin, jnp.zeros((pad_rows, HEAD_DIM), F32)], axis=0)
    return jnp.tile(cos, (1, LANES // HEAD_DIM)), jnp.tile(sin, (1, LANES // HEAD_DIM))


def kernel(x, c, ctx, c_ctx, w_mod, b_mod, ln_g, ln_b, ffn_w_in, ffn_w_out,
           na_w_qkv, na_w_o, na_rpb, wa_w_qkv, wa_w_o, wa_sinks):
    bsz, seq, d = x.shape
    n_ctx = ctx.shape[1]
    depth = w_mod.shape[0]
    d_ff = ffn_w_out.shape[2]
    alpha = (2 * depth) ** 0.25
    t_lat, t_ctx = bsz * seq, bsz * n_ctx
    t_all = t_lat + t_ctx
    nt_lat, nt_all = t_lat // TOKEN_TILE, t_all // TOKEN_TILE
    tiles_per_seq = seq // TOKEN_TILE
    dq = N_HEADS * HEAD_DIM
    dkv = WA_KV_HEADS * HEAD_DIM

    mod_rows = 16
    cc = jnp.concatenate([c, c_ctx[None], jnp.zeros((mod_rows - bsz - 1, d), F32)], axis=0)
    mods = _modulation(cc, w_mod, b_mod)

    w_in, w_out = ffn_w_in[0, 0].astype(BF16), ffn_w_out[0, 0].astype(BF16)
    ffn_w_specs = [_resident((d, 2 * d_ff)), _resident((d_ff, d))]
    na_qkv = na_w_qkv.astype(BF16)
    wa_qkv = wa_w_qkv.astype(BF16)
    na_o = na_w_o.astype(BF16)
    wa_o = wa_w_o.astype(BF16)
    h_parts = [x.reshape(t_lat, d), ctx.reshape(t_ctx, d)]
    ctx_blk0 = t_lat // n_ctx

    for i in range(depth):
        last = i == depth - 1
        is_na = i % 2 == 0
        j = i // 2
        mod_spec = _mod_spec(i, nt_lat, tiles_per_seq, bsz, d)
        ln_spec = pl.BlockSpec((None, 3, d), lambda t, i=i: (i, 0, 0))

        h_specs = _split_tok_specs(d, nt_lat) if len(h_parts) == 2 else [_tok_spec(d)]
        c_in, c_out, c_shape = _cast_next_specs(i, 1, d, d_ff, nt_all)
        h, w_in, w_out = pl.pallas_call(
            functools.partial(_ffn_body, alpha=alpha, n_x=len(h_parts), n_lat_tiles=nt_lat, cast_next=True),
            grid=(nt_all,),
            in_specs=[mod_spec, *h_specs, *ffn_w_specs, ln_spec, ln_spec, *c_in],
            out_specs=[_tok_spec(d), *c_out],
            out_shape=[jax.ShapeDtypeStruct((t_all, d), F32), *c_shape],
            compiler_params=_cparams(1),
            name=f"ffn_pre_{i}",
        )(mods, *h_parts, w_in, w_out, ln_g, ln_b, ffn_w_in, ffn_w_out)

        if is_na:
            q, k, v = pl.pallas_call(
                _qkv_na_body,
                grid=(nt_all,),
                in_specs=[mod_spec, _tok_spec(d), _resident((d, 3 * dq), (j,))],
                out_specs=[_tok_spec(dq)] * 3,
                out_shape=[jax.ShapeDtypeStruct((t_all, dq), BF16)] * 3,
                compiler_params=_cparams(1),
                name=f"qkv_na_{i}",
            )(mods, h, na_qkv)
            att = _na_attention(q, k, v, _na_bias_table(na_rpb[j]), bsz, seq, n_ctx, f"attn_na_{i}")
            w_o, w_o_spec = na_o, _resident((dq, d), (j,))
        else:
            nkv = WA_KV_HEADS * LANES
            cos, sin = _rope_tables(seq, TOKEN_TILE)
            rope_spec = pl.BlockSpec(
                (TOKEN_TILE, LANES), lambda t: (jnp.where(t < nt_lat, t % tiles_per_seq, tiles_per_seq), 0))
            q, k, v = pl.pallas_call(
                _qkv_wa_body,
                grid=(nt_all,),
                in_specs=[mod_spec, _tok_spec(d), rope_spec, rope_spec, _resident((d, dq + 2 * dkv), (j,))],
                out_specs=[_tok_spec(dq), _tok_spec(nkv), _tok_spec(nkv)],
                out_shape=[jax.ShapeDtypeStruct((t_all, dq), BF16),
                           jax.ShapeDtypeStruct((t_all, nkv), BF16),
                           jax.ShapeDtypeStruct((t_all, nkv), BF16)],
                compiler_params=_cparams(1),
                name=f"qkv_wa_{i}",
            )(mods, h, cos, sin, wa_qkv)
            att = _wa_attention(wa_sinks[j], q, k, v, bsz, seq, n_ctx, f"attn_wa_{i}")
            w_o, w_o_spec = wa_o, _resident((dq, d), (j,))

        att_parts = [att]
        if not last:
            ctx_spec = pl.BlockSpec((n_ctx, dq), lambda b: (ctx_blk0 + b, 0))
            att_parts.append(pl.pallas_call(
                _ctx_body,
                grid=(bsz,),
                in_specs=[ctx_spec] * 3,
                out_specs=pl.BlockSpec((n_ctx, dq), lambda b: (b, 0)),
                out_shape=jax.ShapeDtypeStruct((t_ctx, dq), BF16),
                compiler_params=_cparams(1),
                name=f"attn_ctx_{i}",
            )(q, k, v))

        n_tiles = nt_lat if last else nt_all
        att_specs = _split_tok_specs(dq, nt_lat) if len(att_parts) == 2 else [_tok_spec(dq)]
        if last:
            c_in, c_out, c_shape, c_args = [], [], [], []
        else:
            c_in, c_out, c_shape = _cast_next_specs(i + 1, 0, d, d_ff, n_tiles)
            c_args = [ffn_w_in, ffn_w_out]
        h, *next_w = pl.pallas_call(
            functools.partial(_post_body, alpha=alpha, n_a=len(att_parts), n_lat_tiles=nt_lat,
                              cast_next=not last),
            grid=(n_tiles,),
            in_specs=[mod_spec, _tok_spec(d), *att_specs, w_o_spec, *ffn_w_specs, ln_spec, ln_spec, *c_in],
            out_specs=[_tok_spec(d), *c_out],
            out_shape=[jax.ShapeDtypeStruct((n_tiles * TOKEN_TILE, d), F32), *c_shape],
            compiler_params=_cparams(1),
            name=f"post_{i}",
        )(mods, h, *att_parts, w_o, w_in, w_out, ln_g, ln_b, *c_args)
        if next_w:
            w_in, w_out = next_w
        h_parts = [h]

    return h.reshape(bsz, seq, d)
```

```python
import functools

import numpy as np
import jax
import jax.numpy as jnp
from jax import lax
from jax.experimental import pallas as pl
from jax.experimental.pallas import tpu as pltpu

F32 = jnp.float32
BF16 = jnp.bfloat16

GRID_W = 64
N_HEADS = 16
HEAD_DIM = 64
NA_KH = 8
NA_KW = 16
WA_KV_HEADS = 4
WA_WINDOW = 128
FFN_RES = 0.5
ROPE_BASE = 10000.0
N_MOD = 9
LN_EPS = 1e-5
NEG_INF = -1e30

LANES = 128
TOKEN_TILE = 512
MOD_COLS_PER_STEP = 3
CAST_ROWS = 32
CAST_ROWS_OUT = 176
FIRST_ROW_CHUNK = 128
ROW_CHUNK = 256
NA_ROWS_PER_STEP = 8
WA_BLOCKS_PER_STEP = 4
ATTN_LOOKAHEAD = 3
LOG2E = 1.4426950408889634
Q_SCALE = HEAD_DIM ** -0.5 * LOG2E
VMEM_LIMIT = 56 * 1024 * 1024


def _cparams(n_grid):
    return pltpu.CompilerParams(
        dimension_semantics=("arbitrary",) * n_grid, vmem_limit_bytes=VMEM_LIMIT)


def _resident(shape, lead=()):
    lead = tuple(lead)
    nd = len(shape)
    return pl.BlockSpec((None,) * len(lead) + tuple(shape), lambda *_: lead + (0,) * nd,
                        pipeline_mode=pl.Buffered(1))


def _tokens(refs, n_lat_tiles, rows):
    if len(refs) == 1:
        return refs[0][rows, :]
    return jnp.where(pl.program_id(0) < n_lat_tiles, refs[0][rows, :], refs[1][rows, :])


def _row_chunks(n):
    bounds = [0] + list(range(FIRST_ROW_CHUNK, n, ROW_CHUNK)) + [n]
    return [slice(a, b) for a, b in zip(bounds[:-1], bounds[1:])]


def _layer_norm(z, g, b):
    mu = jnp.mean(z, axis=-1, keepdims=True)
    var = jnp.mean(z * z, axis=-1, keepdims=True) - mu * mu
    return (z - mu) * lax.rsqrt(var + LN_EPS) * g + b


def _swiglu(u_bf16, win_ref, wout_ref):
    d_ff = wout_ref.shape[0]
    a = jnp.dot(u_bf16, win_ref[:, :d_ff], preferred_element_type=F32)
    v = jnp.dot(u_bf16, win_ref[:, d_ff:], preferred_element_type=F32)
    g = (a * jax.nn.sigmoid(a) * v).astype(BF16)
    return jnp.dot(g, wout_ref[...], preferred_element_type=F32)


def _ffn_residual(x, mod, base, win_ref, wout_ref, alpha):
    shift, scale, gate = mod[base:base + 1], mod[base + 1:base + 2], mod[base + 2:base + 3]
    u = (x * (1.0 + scale) + shift).astype(BF16)
    y = _swiglu(u, win_ref, wout_ref)
    return alpha * x + (FFN_RES * gate) * y


def _mod_body(c_ref, w_ref, b_ref, o_ref):
    c = c_ref[...]
    s = (c * jax.nn.sigmoid(c)).astype(BF16)
    o_ref[0] = jnp.dot(s, w_ref[0].astype(BF16), preferred_element_type=F32) + b_ref[0]


def _modulation(cc, w_mod, b_mod):
    depth, d, nd = w_mod.shape
    r = cc.shape[0]
    cols = MOD_COLS_PER_STEP * d
    out = pl.pallas_call(
        _mod_body,
        grid=(depth, nd // cols),
        in_specs=[
            pl.BlockSpec((r, d), lambda i, j: (0, 0)),
            pl.BlockSpec((1, d, cols), lambda i, j: (i, 0, j)),
            pl.BlockSpec((1, 1, cols), lambda i, j: (i, 0, j)),
        ],
        out_specs=pl.BlockSpec((1, r, cols), lambda i, j: (i, 0, j)),
        out_shape=jax.ShapeDtypeStruct((depth, r, nd), F32),
        compiler_params=_cparams(2),
        name="modulation",
    )(cc, w_mod, b_mod.reshape(depth, 1, nd))
    return out.reshape(depth, r, nd // d, d)


def _split_cast_refs(refs, cast_next):
    if not cast_next:
        return refs, None
    return refs[:-5] + (refs[-3],), (refs[-5], refs[-4], refs[-2], refs[-1])


def _cast_next_weights(cast_refs):
    if cast_refs is not None:
        src_in, src_out, dst_in, dst_out = cast_refs
        dst_in[...] = src_in[...].astype(BF16)
        dst_out[...] = src_out[...].astype(BF16)


def _ffn_body(mod_ref, *refs, alpha, n_x, n_lat_tiles, cast_next):
    refs, cast_refs = _split_cast_refs(refs, cast_next)
    x_refs, (win_ref, wout_ref, g_ref, b_ref, o_ref) = refs[:n_x], refs[n_x:]
    _cast_next_weights(cast_refs)
    for rows in _row_chunks(o_ref.shape[0]):
        z = _ffn_residual(_tokens(x_refs, n_lat_tiles, rows), mod_ref[0], 0, win_ref, wout_ref, alpha)
        o_ref[rows, :] = _layer_norm(z, g_ref[0:1], b_ref[0:1])


def _post_body(mod_ref, h_ref, *refs, alpha, n_a, n_lat_tiles, cast_next):
    refs, cast_refs = _split_cast_refs(refs, cast_next)
    a_refs, (wo_ref, win_ref, wout_ref, g_ref, b_ref, o_ref) = refs[:n_a], refs[n_a:]
    _cast_next_weights(cast_refs)

    def mixer_out(rows):
        y = jnp.dot(_tokens(a_refs, n_lat_tiles, rows), wo_ref[...], preferred_element_type=F32)
        return _layer_norm(alpha * h_ref[rows, :] + mod_ref[0][5:6] * y, g_ref[1:2], b_ref[1:2])

    chunks = _row_chunks(o_ref.shape[0])
    nxt = mixer_out(chunks[0])
    for c, rows in enumerate(chunks):
        cur = nxt
        if c + 1 < len(chunks):
            nxt = mixer_out(chunks[c + 1])
        z = _ffn_residual(cur, mod_ref[0], 6, win_ref, wout_ref, alpha)
        o_ref[rows, :] = _layer_norm(z, g_ref[2:3], b_ref[2:3])


def _qkv_na_body(mod_ref, h_ref, w_ref, q_ref, k_ref, v_ref):
    mod = mod_ref[0]
    d = h_ref.shape[1]
    for rows in _row_chunks(h_ref.shape[0]):
        u = (h_ref[rows, :] * (1.0 + mod[4:5]) + mod[3:4]).astype(BF16)
        r = jnp.dot(u, w_ref[...], preferred_element_type=F32)
        q_ref[rows, :] = (r[:, :d] * Q_SCALE).astype(BF16)
        k_ref[rows, :] = r[:, d:2 * d].astype(BF16)
        v_ref[rows, :] = r[:, 2 * d:].astype(BF16)


def _rope_lanes(x, cos, sin_signed, first_half):
    partner = jnp.where(first_half, pltpu.roll(x, LANES - 16, 1), pltpu.roll(x, 16, 1))
    return x * cos + partner * sin_signed


def _qkv_wa_body(mod_ref, h_ref, cos_ref, sin_ref, w_ref, q_ref, k_ref, v_ref):
    mod = mod_ref[0]
    d = h_ref.shape[1]
    nk = k_ref.shape[1]
    n_src = nk // (2 * LANES)
    for rows in _row_chunks(h_ref.shape[0]):
        u = (h_ref[rows, :] * (1.0 + mod[4:5]) + mod[3:4]).astype(BF16)
        r = jnp.dot(u, w_ref[...], preferred_element_type=F32)
        cos, sin = cos_ref[rows, :], sin_ref[rows, :]
        first_half = lax.broadcasted_iota(jnp.int32, cos.shape, 1) % 32 < 16
        for j in range(d // LANES):
            sl = slice(j * LANES, (j + 1) * LANES)
            q_ref[rows, sl] = (_rope_lanes(r[:, sl], cos, sin, first_half) * Q_SCALE).astype(BF16)
        lo = lax.broadcasted_iota(jnp.int32, cos.shape, 1) < HEAD_DIM
        for dst_ref, base, rope in ((k_ref, d, True), (v_ref, d + n_src * LANES, False)):
            for j in range(n_src):
                x = r[:, base + j * LANES:base + (j + 1) * LANES]
                if rope:
                    x = _rope_lanes(x, cos, sin, first_half)
                swapped = pltpu.roll(x, HEAD_DIM, 1)
                dst_ref[rows, 2 * j * LANES:(2 * j + 1) * LANES] = jnp.where(lo, x, swapped).astype(BF16)
                dst_ref[rows, (2 * j + 1) * LANES:(2 * j + 2) * LANES] = jnp.where(lo, swapped, x).astype(BF16)


def _mod_spec(layer, n_lat_tiles, tiles_per_seq, ctx_row, d):
    return pl.BlockSpec(
        (None, 1, N_MOD, d),
        lambda t: (layer, jnp.where(t < n_lat_tiles, t // tiles_per_seq, ctx_row), 0, 0))


def _tok_spec(width):
    return pl.BlockSpec((TOKEN_TILE, width), lambda t: (t, 0))


def _cast_next_specs(layer, half, d, d_ff, n_steps):
    n_in, n_out = d // CAST_ROWS, d_ff // CAST_ROWS_OUT
    assert n_steps >= max(n_in, n_out)
    in_specs = [
        pl.BlockSpec((None, None, CAST_ROWS, 2 * d_ff), lambda t: (layer, half, jnp.minimum(t, n_in - 1), 0)),
        pl.BlockSpec((None, None, CAST_ROWS_OUT, d), lambda t: (layer, half, jnp.minimum(t, n_out - 1), 0))]
    out_specs = [
        pl.BlockSpec((CAST_ROWS, 2 * d_ff), lambda t: (jnp.minimum(t, n_in - 1), 0)),
        pl.BlockSpec((CAST_ROWS_OUT, d), lambda t: (jnp.minimum(t, n_out - 1), 0))]
    out_shape = [jax.ShapeDtypeStruct((d, 2 * d_ff), BF16), jax.ShapeDtypeStruct((d_ff, d), BF16)]
    return in_specs, out_specs, out_shape


def _split_tok_specs(width, n_lat_tiles):
    return [pl.BlockSpec((TOKEN_TILE, width), lambda t: (jnp.minimum(t, n_lat_tiles - 1), 0)),
            pl.BlockSpec((TOKEN_TILE, width), lambda t: (jnp.maximum(t - n_lat_tiles, 0), 0))]


def _split_heads(x):
    lo = lax.broadcasted_iota(jnp.int32, x.shape, 1) < HEAD_DIM
    zero = jnp.zeros_like(x)
    return jnp.concatenate([jnp.where(lo, x, zero), jnp.where(lo, zero, x)], axis=0)


def _merge_heads(o):
    n = o.shape[0] // 2
    lo = lax.broadcasted_iota(jnp.int32, (n, LANES), 1) < HEAD_DIM
    return jnp.where(lo, o[:n], o[n:])


def _qk(q, k):
    return lax.dot_general(q, k, (((1,), (1,)), ((), ())), preferred_element_type=F32)


def _lane_fold(op, *xs):
    parts = [x[:, i:i + LANES] for x in xs for i in range(0, x.shape[1], LANES)]
    return functools.reduce(op, parts)


def _softmax_numerators(s_win, s_ctx, sink):
    x_win, x_ctx = s_win.astype(BF16), s_ctx.astype(BF16)
    m = jnp.max(_lane_fold(jnp.maximum, x_win, x_ctx).astype(F32), axis=-1, keepdims=True)
    if sink is not None:
        m = jnp.maximum(m, sink)
    m_bf16 = m.astype(BF16)
    return jnp.exp2(x_win - m_bf16), jnp.exp2(x_ctx - m_bf16), m_bf16.astype(F32)


def _with_ones(v):
    return jnp.concatenate([v, jnp.ones_like(v)], axis=1)


def _normalise(o_sum, extra):
    l = o_sum[:, LANES:LANES + 1]
    if extra is not None:
        l = l + extra
    return o_sum[:, :LANES] / l


def _pipelined(n, scores, finish):
    pending = [scores(i) for i in range(min(ATTN_LOOKAHEAD, n))]
    for i in range(n):
        if i + ATTN_LOOKAHEAD < n:
            pending.append(scores(i + ATTN_LOOKAHEAD))
        finish(i, *pending.pop(0))


def _na_body(q_ref, k_ref, v_ref, kc_ref, vc_ref, t2_ref, o_ref):
    rows = k_ref.shape[0] // GRID_W
    n_win = NA_KH * GRID_W
    n_rows = q_ref.shape[0] // GRID_W
    r0 = pl.program_id(1) * n_rows
    key_rows = [jnp.clip(r0 + i - NA_KH // 2, 0, rows - NA_KH) for i in range(n_rows)]
    k_offs = [pl.multiple_of(rs * GRID_W, GRID_W) for rs in key_rows]
    shifts = [rs - (r0 + i) + NA_KH - 1 for i, rs in enumerate(key_rows)]
    m_row = 2 * GRID_W

    def scores(p):
        sl = slice(p * LANES, (p + 1) * LANES)
        qs = [_split_heads(q_ref[i * GRID_W:(i + 1) * GRID_W, sl]) for i in range(n_rows)]
        s_win = jnp.concatenate(
            [_qk(qs[i], k_ref[pl.ds(k_offs[i], n_win), sl]) for i in range(n_rows)], axis=0)
        return s_win, _qk(jnp.concatenate(qs, axis=0), kc_ref[:, sl])

    def finish(p, s_win, s_ctx):
        sl = slice(p * LANES, (p + 1) * LANES)
        bias = jnp.concatenate(
            [jnp.concatenate(
                [jnp.concatenate([t2_ref[2 * p, sh + 2 * jj], t2_ref[2 * p + 1, sh + 2 * jj]], axis=0)
                 for jj in range(NA_KH // 2)], axis=1) for sh in shifts], axis=0)
        p_win, p_ctx, _ = _softmax_numerators(s_win + bias, s_ctx, None)
        o_win = jnp.concatenate(
            [jnp.dot(p_win[i * m_row:(i + 1) * m_row], _with_ones(v_ref[pl.ds(k_offs[i], n_win), sl]),
                     preferred_element_type=F32) for i in range(n_rows)], axis=0)
        o = _normalise(o_win + jnp.dot(p_ctx, _with_ones(vc_ref[:, sl]), preferred_element_type=F32), None)
        for i in range(n_rows):
            o_ref[i * GRID_W:(i + 1) * GRID_W, sl] = _merge_heads(o[i * m_row:(i + 1) * m_row]).astype(BF16)

    _pipelined(N_HEADS // 2, scores, finish)


def _ctx_body(q_ref, k_ref, v_ref, o_ref):
    for p in range(N_HEADS // 2):
        sl = slice(p * LANES, (p + 1) * LANES)
        qs = _split_heads(q_ref[:, sl])
        s = _qk(qs, k_ref[:, sl])
        m = jnp.max(s, axis=-1, keepdims=True)
        e = jnp.exp2(s - m)
        l = jnp.sum(e, axis=-1, keepdims=True)
        o = jnp.dot(e.astype(BF16), v_ref[:, sl], preferred_element_type=F32)
        o_ref[:, sl] = _merge_heads(o / l).astype(BF16)


def _wa_body(sink_ref, q_ref, k_ref, v_ref, kc_ref, vc_ref, o_ref):
    seq = k_ref.shape[0]
    blk = WA_WINDOW
    n_blk = q_ref.shape[0] // blk
    n_win = 3 * blk
    qb0 = pl.program_id(1) * n_blk
    starts = [pl.multiple_of(jnp.clip((qb0 + b - 1) * blk, 0, seq - n_win), blk) for b in range(n_blk)]
    d_row = lax.broadcasted_iota(jnp.int32, (blk, n_win), 0) - lax.broadcasted_iota(jnp.int32, (blk, n_win), 1)
    bands = []
    for b in range(n_blk):
        dist = d_row + ((qb0 + b) * blk - starts[b])
        band = jnp.where(jnp.abs(dist) <= WA_WINDOW, 0.0, NEG_INF).astype(F32)
        bands += [band, band]
    band = jnp.concatenate(bands, axis=0)
    pairs_per_kv = N_HEADS // WA_KV_HEADS // 2
    m_blk = 2 * blk

    def scores(p):
        kv = slice(p // pairs_per_kv * LANES, (p // pairs_per_kv + 1) * LANES)
        qs = [_split_heads(q_ref[b * blk:(b + 1) * blk, p * LANES:(p + 1) * LANES]) for b in range(n_blk)]
        s_win = jnp.concatenate(
            [_qk(qs[b], k_ref[pl.ds(starts[b], n_win), kv]) for b in range(n_blk)], axis=0)
        return s_win, _qk(jnp.concatenate(qs, axis=0), kc_ref[:, kv])

    def finish(p, s_win, s_ctx):
        kv = slice(p // pairs_per_kv * LANES, (p // pairs_per_kv + 1) * LANES)
        sink = jnp.concatenate(
            [jnp.full((blk, 1), sink_ref[2 * p + j] * LOG2E, F32) for j in range(2)] * n_blk, axis=0)
        p_win, p_ctx, m = _softmax_numerators(s_win + band, s_ctx, sink)
        o_win = jnp.concatenate(
            [jnp.dot(p_win[b * m_blk:(b + 1) * m_blk], _with_ones(v_ref[pl.ds(starts[b], n_win), kv]),
                     preferred_element_type=F32) for b in range(n_blk)], axis=0)
        o = _normalise(o_win + jnp.dot(p_ctx, _with_ones(vc_ref[:, kv]), preferred_element_type=F32),
                       jnp.exp2(sink - m))
        for b in range(n_blk):
            o_ref[b * blk:(b + 1) * blk, p * LANES:(p + 1) * LANES] = _merge_heads(
                o[b * m_blk:(b + 1) * m_blk]).astype(BF16)

    _pipelined(N_HEADS // 2, scores, finish)


def _attention_specs(q_rows, dq, nkv, bsz, seq, n_ctx):
    steps = seq // q_rows
    ctx_blk0 = bsz * seq // n_ctx
    q_spec = pl.BlockSpec((q_rows, dq), lambda b, r: (b * steps + r, 0))
    kv_spec = pl.BlockSpec((seq, nkv), lambda b, r: (b, 0))
    ctx_spec = pl.BlockSpec((n_ctx, nkv), lambda b, r: (ctx_blk0 + b, 0))
    return (bsz, steps), q_spec, [kv_spec, kv_spec, ctx_spec, ctx_spec]


def _na_attention(q, k, v, t2, bsz, seq, n_ctx, name):
    dq = q.shape[1]
    grid, q_spec, kv_specs = _attention_specs(NA_ROWS_PER_STEP * GRID_W, dq, dq, bsz, seq, n_ctx)
    return pl.pallas_call(
        _na_body,
        grid=grid,
        in_specs=[q_spec, *kv_specs, _resident(t2.shape)],
        out_specs=q_spec,
        out_shape=jax.ShapeDtypeStruct((bsz * seq, dq), BF16),
        compiler_params=_cparams(2),
        name=name,
    )(q, k, v, k, v, t2)


def _wa_attention(sinks, q, k, v, bsz, seq, n_ctx, name):
    dq = q.shape[1]
    grid, q_spec, kv_specs = _attention_specs(WA_BLOCKS_PER_STEP * WA_WINDOW, dq, k.shape[1], bsz, seq, n_ctx)
    return pl.pallas_call(
        _wa_body,
        grid=grid,
        in_specs=[pl.BlockSpec(memory_space=pltpu.SMEM), q_spec, *kv_specs],
        out_specs=q_spec,
        out_shape=jax.ShapeDtypeStruct((bsz * seq, dq), BF16),
        compiler_params=_cparams(2),
        name=name,
    )(sinks, q, k, v, k, v)


def _na_bias_table(rpb):
    qcol = np.arange(GRID_W)[:, None]
    kcol = np.arange(GRID_W)[None, :]
    win_start = np.clip(qcol - NA_KW // 2, 0, GRID_W - NA_KW)
    valid = (kcol >= win_start) & (kcol < win_start + NA_KW)
    off = np.clip(kcol - qcol, -(NA_KW - 1), NA_KW - 1) + NA_KW - 1
    n_off = 2 * NA_KW - 1
    onehot = (off[None] == np.arange(n_off)[:, None, None]).astype(np.float32)
    pair = np.zeros((2 * n_off, GRID_W, 2 * GRID_W), np.float32)
    pair[:n_off, :, :GRID_W] = onehot
    pair[n_off:, :, GRID_W:] = onehot
    rpb2 = jnp.concatenate([rpb[:, :-1], rpb[:, 1:]], axis=-1).astype(F32)
    t = jnp.einsum("hdo,oqk->hdqk", rpb2, pair, precision=lax.Precision.HIGHEST)
    return jnp.where(np.concatenate([valid, valid], axis=1), t * LOG2E, NEG_INF)


def _rope_tables(seq, pad_rows):
    t = jnp.arange(seq, dtype=jnp.int32)
    n = HEAD_DIM // 4
    inv_freq = ROPE_BASE ** (-jnp.arange(n, dtype=F32) / n)
    ang_r = (t // GRID_W).astype(F32)[:, None] * inv_freq[None, :]
    ang_c = (t % GRID_W).astype(F32)[:, None] * inv_freq[None, :]
    ang = jnp.concatenate([ang_r, ang_r, ang_c, ang_c], axis=1)
    sign = np.tile(np.concatenate([-np.ones(n), np.ones(n)]), 2).astype(np.float32)
    cos = jnp.cos(ang)
    sin = jnp.sin(ang) * sign
    cos = jnp.concatenate([cos, jnp.ones((pad_rows, HEAD_DIM), F32)], axis=0)
    sin = jnp.concatenate([sin, jnp.zeros((pad_rows, HEAD_DIM), F32)], axis=0)
    return jnp.tile(cos, (1, LANES // HEAD_DIM)), jnp.tile(sin, (1, LANES // HEAD_DIM))


def kernel(x, c, ctx, c_ctx, w_mod, b_mod, ln_g, ln_b, ffn_w_in, ffn_w_out,
           na_w_qkv, na_w_o, na_rpb, wa_w_qkv, wa_w_o, wa_sinks):
    bsz, seq, d = x.shape
    n_ctx = ctx.shape[1]
    depth = w_mod.shape[0]
    d_ff = ffn_w_out.shape[2]
    alpha = (2 * depth) ** 0.25
    t_lat, t_ctx = bsz * seq, bsz * n_ctx
    t_all = t_lat + t_ctx
    nt_lat, nt_all = t_lat // TOKEN_TILE, t_all // TOKEN_TILE
    tiles_per_seq = seq // TOKEN_TILE
    dq = N_HEADS * HEAD_DIM
    dkv = WA_KV_HEADS * HEAD_DIM

    mod_rows = 16
    cc = jnp.concatenate([c, c_ctx[None], jnp.zeros((mod_rows - bsz - 1, d), F32)], axis=0)
    mods = _modulation(cc, w_mod, b_mod)

    w_in, w_out = ffn_w_in[0, 0].astype(BF16), ffn_w_out[0, 0].astype(BF16)
    ffn_w_specs = [_resident((d, 2 * d_ff)), _resident((d_ff, d))]
    na_qkv = na_w_qkv.astype(BF16)
    wa_qkv = wa_w_qkv.astype(BF16)
    na_o = na_w_o.astype(BF16)
    wa_o = wa_w_o.astype(BF16)
    h_parts = [x.reshape(t_lat, d), ctx.reshape(t_ctx, d)]
    ctx_blk0 = t_lat // n_ctx

    for i in range(depth):
        last = i == depth - 1
        is_na = i % 2 == 0
        j = i // 2
        mod_spec = _mod_spec(i, nt_lat, tiles_per_seq, bsz, d)
        ln_spec = pl.BlockSpec((None, 3, d), lambda t, i=i: (i, 0, 0))

        h_specs = _split_tok_specs(d, nt_lat) if len(h_parts) == 2 else [_tok_spec(d)]
        c_in, c_out, c_shape = _cast_next_specs(i, 1, d, d_ff, nt_all)
        h, w_in, w_out = pl.pallas_call(
            functools.partial(_ffn_body, alpha=alpha, n_x=len(h_parts), n_lat_tiles=nt_lat, cast_next=True),
            grid=(nt_all,),
            in_specs=[mod_spec, *h_specs, *ffn_w_specs, ln_spec, ln_spec, *c_in],
            out_specs=[_tok_spec(d), *c_out],
            out_shape=[jax.ShapeDtypeStruct((t_all, d), F32), *c_shape],
            compiler_params=_cparams(1),
            name=f"ffn_pre_{i}",
        )(mods, *h_parts, w_in, w_out, ln_g, ln_b, ffn_w_in, ffn_w_out)

        if is_na:
            q, k, v = pl.pallas_call(
                _qkv_na_body,
                grid=(nt_all,),
                in_specs=[mod_spec, _tok_spec(d), _resident((d, 3 * dq), (j,))],
                out_specs=[_tok_spec(dq)] * 3,
                out_shape=[jax.ShapeDtypeStruct((t_all, dq), BF16)] * 3,
                compiler_params=_cparams(1),
                name=f"qkv_na_{i}",
            )(mods, h, na_qkv)
            att = _na_attention(q, k, v, _na_bias_table(na_rpb[j]), bsz, seq, n_ctx, f"attn_na_{i}")
            w_o, w_o_spec = na_o, _resident((dq, d), (j,))
        else:
            nkv = WA_KV_HEADS * LANES
            cos, sin = _rope_tables(seq, TOKEN_TILE)
            rope_spec = pl.BlockSpec(
                (TOKEN_TILE, LANES), lambda t: (jnp.where(t < nt_lat, t % tiles_per_seq, tiles_per_seq), 0))
            q, k, v = pl.pallas_call(
                _qkv_wa_body,
                grid=(nt_all,),
                in_specs=[mod_spec, _tok_spec(d), rope_spec, rope_spec, _resident((d, dq + 2 * dkv), (j,))],
                out_specs=[_tok_spec(dq), _tok_spec(nkv), _tok_spec(nkv)],
                out_shape=[jax.ShapeDtypeStruct((t_all, dq), BF16),
                           jax.ShapeDtypeStruct((t_all, nkv), BF16),
                           jax.ShapeDtypeStruct((t_all, nkv), BF16)],
                compiler_params=_cparams(1),
                name=f"qkv_wa_{i}",
            )(mods, h, cos, sin, wa_qkv)
            att = _wa_attention(wa_sinks[j], q, k, v, bsz, seq, n_ctx, f"attn_wa_{i}")
            w_o, w_o_spec = wa_o, _resident((dq, d), (j,))

        att_parts = [att]
        if not last:
            ctx_spec = pl.BlockSpec((n_ctx, dq), lambda b: (ctx_blk0 + b, 0))
            att_parts.append(pl.pallas_call(
                _ctx_body,
                grid=(bsz,),
                in_specs=[ctx_spec] * 3,
                out_specs=pl.BlockSpec((n_ctx, dq), lambda b: (b, 0)),
                out_shape=jax.ShapeDtypeStruct((t_ctx, dq), BF16),
                compiler_params=_cparams(1),
                name=f"attn_ctx_{i}",
            )(q, k, v))

        n_tiles = nt_lat if last else nt_all
        att_specs = _split_tok_specs(dq, nt_lat) if len(att_parts) == 2 else [_tok_spec(dq)]
        if last:
            c_in, c_out, c_shape, c_args = [], [], [], []
        else:
            c_in, c_out, c_shape = _cast_next_specs(i + 1, 0, d, d_ff, n_tiles)
            c_args = [ffn_w_in, ffn_w_out]
        h, *next_w = pl.pallas_call(
            functools.partial(_post_body, alpha=alpha, n_a=len(att_parts), n_lat_tiles=nt_lat,
                              cast_next=not last),
            grid=(n_tiles,),
            in_specs=[mod_spec, _tok_spec(d), *att_specs, w_o_spec, *ffn_w_specs, ln_spec, ln_spec, *c_in],
            out_specs=[_tok_spec(d), *c_out],
            out_shape=[jax.ShapeDtypeStruct((n_tiles * TOKEN_TILE, d), F32), *c_shape],
            compiler_params=_cparams(1),
            name=f"post_{i}",
        )(mods, h, *att_parts, w_o, w_in, w_out, ln_g, ln_b, *c_args)
        if next_w:
            w_in, w_out = next_w
        h_parts = [h]

    return h.reshape(bsz, seq, d)
```

```python
import functools

import numpy as np
import jax
import jax.numpy as jnp
from jax import lax
from jax.experimental import pallas as pl
from jax.experimental.pallas import tpu as pltpu

F32 = jnp.float32
BF16 = jnp.bfloat16

GRID_W = 64
N_HEADS = 16
HEAD_DIM = 64
NA_KH = 8
NA_KW = 16
WA_KV_HEADS = 4
WA_WINDOW = 128
FFN_RES = 0.5
ROPE_BASE = 10000.0
N_MOD = 9
LN_EPS = 1e-5
NEG_INF = -1e30

LANES = 128
TOKEN_TILE = 512
MOD_COLS_PER_STEP = 3
CAST_ROWS = 32
CAST_ROWS_OUT = 176
FIRST_ROW_CHUNK = 128
ROW_CHUNK = 256
NA_ROWS_PER_STEP = 8
WA_BLOCKS_PER_STEP = 4
NA_LOOKAHEAD = 2
WA_LOOKAHEAD = 4
LOG2E = 1.4426950408889634
Q_SCALE = HEAD_DIM ** -0.5 * LOG2E
VMEM_LIMIT = 56 * 1024 * 1024


def _cparams(n_grid):
    return pltpu.CompilerParams(
        dimension_semantics=("arbitrary",) * n_grid, vmem_limit_bytes=VMEM_LIMIT)


def _resident(shape, lead=()):
    lead = tuple(lead)
    nd = len(shape)
    return pl.BlockSpec((None,) * len(lead) + tuple(shape), lambda *_: lead + (0,) * nd,
                        pipeline_mode=pl.Buffered(1))


def _tokens(refs, n_lat_tiles, rows):
    if len(refs) == 1:
        return refs[0][rows, :]
    return jnp.where(pl.program_id(0) < n_lat_tiles, refs[0][rows, :], refs[1][rows, :])


def _row_chunks(n):
    bounds = [0] + list(range(FIRST_ROW_CHUNK, n, ROW_CHUNK)) + [n]
    return [slice(a, b) for a, b in zip(bounds[:-1], bounds[1:])]


def _layer_norm(z, g, b):
    mu = jnp.mean(z, axis=-1, keepdims=True)
    var = jnp.mean(z * z, axis=-1, keepdims=True) - mu * mu
    return (z - mu) * lax.rsqrt(var + LN_EPS) * g + b


def _swiglu(u_bf16, win_ref, wout_ref):
    d_ff = wout_ref.shape[0]
    a = jnp.dot(u_bf16, win_ref[:, :d_ff], preferred_element_type=F32)
    v = jnp.dot(u_bf16, win_ref[:, d_ff:], preferred_element_type=F32)
    g = (a * jax.nn.sigmoid(a) * v).astype(BF16)
    return jnp.dot(g, wout_ref[...], preferred_element_type=F32)


def _ffn_residual(x, mod, base, win_ref, wout_ref, alpha):
    shift, scale, gate = mod[base:base + 1], mod[base + 1:base + 2], mod[base + 2:base + 3]
    u = (x * (1.0 + scale) + shift).astype(BF16)
    y = _swiglu(u, win_ref, wout_ref)
    return alpha * x + (FFN_RES * gate) * y


def _mod_body(c_ref, w_ref, b_ref, o_ref):
    c = c_ref[...]
    s = (c * jax.nn.sigmoid(c)).astype(BF16)
    o_ref[0] = jnp.dot(s, w_ref[0].astype(BF16), preferred_element_type=F32) + b_ref[0]


def _modulation(cc, w_mod, b_mod):
    depth, d, nd = w_mod.shape
    r = cc.shape[0]
    cols = MOD_COLS_PER_STEP * d
    out = pl.pallas_call(
        _mod_body,
        grid=(depth, nd // cols),
        in_specs=[
            pl.BlockSpec((r, d), lambda i, j: (0, 0)),
            pl.BlockSpec((1, d, cols), lambda i, j: (i, 0, j)),
            pl.BlockSpec((1, 1, cols), lambda i, j: (i, 0, j)),
        ],
        out_specs=pl.BlockSpec((1, r, cols), lambda i, j: (i, 0, j)),
        out_shape=jax.ShapeDtypeStruct((depth, r, nd), F32),
        compiler_params=_cparams(2),
        name="modulation",
    )(cc, w_mod, b_mod.reshape(depth, 1, nd))
    return out.reshape(depth, r, nd // d, d)


def _split_cast_refs(refs, n_cast):
    if not n_cast:
        return refs, ()
    main = refs[:-(2 * n_cast + 1)] + (refs[-(n_cast + 1)],)
    return main, tuple(zip(refs[-(2 * n_cast + 1):-(n_cast + 1)], refs[-n_cast:]))


def _cast_weights(cast_refs):
    for src, dst in cast_refs:
        dst[...] = src[...].astype(BF16)


def _ffn_body(mod_ref, *refs, alpha, n_x, n_lat_tiles, n_cast):
    refs, cast_refs = _split_cast_refs(refs, n_cast)
    x_refs, (win_ref, wout_ref, g_ref, b_ref, o_ref) = refs[:n_x], refs[n_x:]
    _cast_weights(cast_refs)
    for rows in _row_chunks(o_ref.shape[0]):
        z = _ffn_residual(_tokens(x_refs, n_lat_tiles, rows), mod_ref[0], 0, win_ref, wout_ref, alpha)
        o_ref[rows, :] = _layer_norm(z, g_ref[0:1], b_ref[0:1])


def _post_body(mod_ref, h_ref, *refs, alpha, n_a, n_lat_tiles, n_cast):
    refs, cast_refs = _split_cast_refs(refs, n_cast)
    a_refs, (wo_ref, win_ref, wout_ref, g_ref, b_ref, o_ref) = refs[:n_a], refs[n_a:]
    _cast_weights(cast_refs)

    def mixer_out(rows):
        y = jnp.dot(_tokens(a_refs, n_lat_tiles, rows), wo_ref[...], preferred_element_type=F32)
        return _layer_norm(alpha * h_ref[rows, :] + mod_ref[0][5:6] * y, g_ref[1:2], b_ref[1:2])

    chunks = _row_chunks(o_ref.shape[0])
    nxt = mixer_out(chunks[0])
    for c, rows in enumerate(chunks):
        cur = nxt
        if c + 1 < len(chunks):
            nxt = mixer_out(chunks[c + 1])
        z = _ffn_residual(cur, mod_ref[0], 6, win_ref, wout_ref, alpha)
        o_ref[rows, :] = _layer_norm(z, g_ref[2:3], b_ref[2:3])


def _qkv_na_body(mod_ref, h_ref, w_ref, q_ref, k_ref, v_ref):
    mod = mod_ref[0]
    d = h_ref.shape[1]
    for rows in _row_chunks(h_ref.shape[0]):
        u = (h_ref[rows, :] * (1.0 + mod[4:5]) + mod[3:4]).astype(BF16)
        r = jnp.dot(u, w_ref[...], preferred_element_type=F32)
        q_ref[rows, :] = (r[:, :d] * Q_SCALE).astype(BF16)
        k_ref[rows, :] = r[:, d:2 * d].astype(BF16)
        v_ref[rows, :] = r[:, 2 * d:].astype(BF16)


def _rope_lanes(x, cos, sin_signed, first_half):
    partner = jnp.where(first_half, pltpu.roll(x, LANES - 16, 1), pltpu.roll(x, 16, 1))
    return x * cos + partner * sin_signed


def _qkv_wa_body(mod_ref, h_ref, cos_ref, sin_ref, w_ref, q_ref, k_ref, v_ref):
    mod = mod_ref[0]
    d = h_ref.shape[1]
    nk = k_ref.shape[1]
    n_src = nk // (2 * LANES)
    for rows in _row_chunks(h_ref.shape[0]):
        u = (h_ref[rows, :] * (1.0 + mod[4:5]) + mod[3:4]).astype(BF16)
        r = jnp.dot(u, w_ref[...], preferred_element_type=F32)
        cos, sin = cos_ref[rows, :], sin_ref[rows, :]
        first_half = lax.broadcasted_iota(jnp.int32, cos.shape, 1) % 32 < 16
        for j in range(d // LANES):
            sl = slice(j * LANES, (j + 1) * LANES)
            q_ref[rows, sl] = (_rope_lanes(r[:, sl], cos, sin, first_half) * Q_SCALE).astype(BF16)
        lo = lax.broadcasted_iota(jnp.int32, cos.shape, 1) < HEAD_DIM
        for dst_ref, base, rope in ((k_ref, d, True), (v_ref, d + n_src * LANES, False)):
            for j in range(n_src):
                x = r[:, base + j * LANES:base + (j + 1) * LANES]
                if rope:
                    x = _rope_lanes(x, cos, sin, first_half)
                swapped = pltpu.roll(x, HEAD_DIM, 1)
                dst_ref[rows, 2 * j * LANES:(2 * j + 1) * LANES] = jnp.where(lo, x, swapped).astype(BF16)
                dst_ref[rows, (2 * j + 1) * LANES:(2 * j + 2) * LANES] = jnp.where(lo, swapped, x).astype(BF16)


def _mod_spec(layer, n_lat_tiles, tiles_per_seq, ctx_row, d):
    return pl.BlockSpec(
        (None, 1, N_MOD, d),
        lambda t: (layer, jnp.where(t < n_lat_tiles, t // tiles_per_seq, ctx_row), 0, 0))


def _tok_spec(width):
    return pl.BlockSpec((TOKEN_TILE, width), lambda t: (t, 0))


def _cast_job(weight, lead, rows_per_step, n_steps):
    lead = tuple(lead)
    rows, cols = weight.shape[len(lead):]
    n = rows // rows_per_step
    assert rows == n * rows_per_step and n_steps >= n
    in_spec = pl.BlockSpec((None,) * len(lead) + (rows_per_step, cols),
                           lambda t: lead + (jnp.minimum(t, n - 1), 0))
    out_spec = pl.BlockSpec((rows_per_step, cols), lambda t: (jnp.minimum(t, n - 1), 0))
    return in_spec, out_spec, jax.ShapeDtypeStruct((rows, cols), BF16), weight


def _split_tok_specs(width, n_lat_tiles):
    return [pl.BlockSpec((TOKEN_TILE, width), lambda t: (jnp.minimum(t, n_lat_tiles - 1), 0)),
            pl.BlockSpec((TOKEN_TILE, width), lambda t: (jnp.maximum(t - n_lat_tiles, 0), 0))]


def _split_heads(x):
    lo = lax.broadcasted_iota(jnp.int32, x.shape, 1) < HEAD_DIM
    zero = jnp.zeros_like(x)
    return jnp.concatenate([jnp.where(lo, x, zero), jnp.where(lo, zero, x)], axis=0)


def _merge_heads(o):
    n = o.shape[0] // 2
    lo = lax.broadcasted_iota(jnp.int32, (n, LANES), 1) < HEAD_DIM
    return jnp.where(lo, o[:n], o[n:])


def _qk(q, k):
    return lax.dot_general(q, k, (((1,), (1,)), ((), ())), preferred_element_type=F32)


def _lane_fold(op, *xs):
    parts = [x[:, i:i + LANES] for x in xs for i in range(0, x.shape[1], LANES)]
    return functools.reduce(op, parts)


def _softmax_numerators(s_win, s_ctx, sink):
    x_win, x_ctx = s_win.astype(BF16), s_ctx.astype(BF16)
    m = jnp.max(_lane_fold(jnp.maximum, x_win, x_ctx).astype(F32), axis=-1, keepdims=True)
    if sink is not None:
        m = jnp.maximum(m, sink)
    m_bf16 = m.astype(BF16)
    return jnp.exp2(x_win - m_bf16), jnp.exp2(x_ctx - m_bf16), m_bf16.astype(F32)


def _with_ones(v):
    return jnp.concatenate([v, jnp.ones_like(v)], axis=1)


def _normalise(o_sum, extra):
    l = o_sum[:, LANES:LANES + 1]
    if extra is not None:
        l = l + extra
    return o_sum[:, :LANES] / l


def _pipelined(n, scores, finish, lookahead):
    pending = [scores(i) for i in range(min(lookahead, n))]
    for i in range(n):
        if i + lookahead < n:
            pending.append(scores(i + lookahead))
        finish(i, *pending.pop(0))


def _na_body(q_ref, k_ref, v_ref, kc_ref, vc_ref, t2_ref, o_ref):
    rows = k_ref.shape[0] // GRID_W
    n_win = NA_KH * GRID_W
    n_rows = q_ref.shape[0] // GRID_W
    r0 = pl.program_id(1) * n_rows
    key_rows = [jnp.clip(r0 + i - NA_KH // 2, 0, rows - NA_KH) for i in range(n_rows)]
    k_offs = [pl.multiple_of(rs * GRID_W, GRID_W) for rs in key_rows]
    shifts = [rs - (r0 + i) + NA_KH - 1 for i, rs in enumerate(key_rows)]
    m_row = 2 * GRID_W

    def scores(p):
        sl = slice(p * LANES, (p + 1) * LANES)
        qs = [_split_heads(q_ref[i * GRID_W:(i + 1) * GRID_W, sl]) for i in range(n_rows)]
        s_win = jnp.concatenate(
            [_qk(qs[i], k_ref[pl.ds(k_offs[i], n_win), sl]) for i in range(n_rows)], axis=0)
        return s_win, _qk(jnp.concatenate(qs, axis=0), kc_ref[:, sl])

    def finish(p, s_win, s_ctx):
        sl = slice(p * LANES, (p + 1) * LANES)
        bias = jnp.concatenate(
            [jnp.concatenate(
                [jnp.concatenate([t2_ref[2 * p, sh + 2 * jj], t2_ref[2 * p + 1, sh + 2 * jj]], axis=0)
                 for jj in range(NA_KH // 2)], axis=1) for sh in shifts], axis=0)
        p_win, p_ctx, _ = _softmax_numerators(s_win + bias, s_ctx, None)
        o_win = jnp.concatenate(
            [jnp.dot(p_win[i * m_row:(i + 1) * m_row], _with_ones(v_ref[pl.ds(k_offs[i], n_win), sl]),
                     preferred_element_type=F32) for i in range(n_rows)], axis=0)
        o = _normalise(o_win + jnp.dot(p_ctx, _with_ones(vc_ref[:, sl]), preferred_element_type=F32), None)
        for i in range(n_rows):
            o_ref[i * GRID_W:(i + 1) * GRID_W, sl] = _merge_heads(o[i * m_row:(i + 1) * m_row]).astype(BF16)

    _pipelined(N_HEADS // 2, scores, finish, NA_LOOKAHEAD)


def _ctx_body(q_ref, k_ref, v_ref, o_ref):
    for p in range(N_HEADS // 2):
        sl = slice(p * LANES, (p + 1) * LANES)
        qs = _split_heads(q_ref[:, sl])
        s = _qk(qs, k_ref[:, sl])
        m = jnp.max(s, axis=-1, keepdims=True)
        e = jnp.exp2(s - m)
        l = jnp.sum(e, axis=-1, keepdims=True)
        o = jnp.dot(e.astype(BF16), v_ref[:, sl], preferred_element_type=F32)
        o_ref[:, sl] = _merge_heads(o / l).astype(BF16)


def _wa_body(sink_ref, q_ref, k_ref, v_ref, kc_ref, vc_ref, o_ref):
    seq = k_ref.shape[0]
    blk = WA_WINDOW
    n_blk = q_ref.shape[0] // blk
    n_win = 3 * blk
    qb0 = pl.program_id(1) * n_blk
    starts = [pl.multiple_of(jnp.clip((qb0 + b - 1) * blk, 0, seq - n_win), blk) for b in range(n_blk)]
    d_row = lax.broadcasted_iota(jnp.int32, (blk, n_win), 0) - lax.broadcasted_iota(jnp.int32, (blk, n_win), 1)
    bands = []
    for b in range(n_blk):
        dist = d_row + ((qb0 + b) * blk - starts[b])
        band = jnp.where(jnp.abs(dist) <= WA_WINDOW, 0.0, NEG_INF).astype(F32)
        bands += [band, band]
    band = jnp.concatenate(bands, axis=0)
    pairs_per_kv = N_HEADS // WA_KV_HEADS // 2
    m_blk = 2 * blk

    def scores(p):
        kv = slice(p // pairs_per_kv * LANES, (p // pairs_per_kv + 1) * LANES)
        qs = [_split_heads(q_ref[b * blk:(b + 1) * blk, p * LANES:(p + 1) * LANES]) for b in range(n_blk)]
        s_win = jnp.concatenate(
            [_qk(qs[b], k_ref[pl.ds(starts[b], n_win), kv]) for b in range(n_blk)], axis=0)
        return s_win, _qk(jnp.concatenate(qs, axis=0), kc_ref[:, kv])

    def finish(p, s_win, s_ctx):
        kv = slice(p // pairs_per_kv * LANES, (p // pairs_per_kv + 1) * LANES)
        sink = jnp.concatenate(
            [jnp.full((blk, 1), sink_ref[2 * p + j] * LOG2E, F32) for j in range(2)] * n_blk, axis=0)
        p_win, p_ctx, m = _softmax_numerators(s_win + band, s_ctx, sink)
        o_win = jnp.concatenate(
            [jnp.dot(p_win[b * m_blk:(b + 1) * m_blk], _with_ones(v_ref[pl.ds(starts[b], n_win), kv]),
                     preferred_element_type=F32) for b in range(n_blk)], axis=0)
        o = _normalise(o_win + jnp.dot(p_ctx, _with_ones(vc_ref[:, kv]), preferred_element_type=F32),
                       jnp.exp2(sink - m))
        for b in range(n_blk):
            o_ref[b * blk:(b + 1) * blk, p * LANES:(p + 1) * LANES] = _merge_heads(
                o[b * m_blk:(b + 1) * m_blk]).astype(BF16)

    _pipelined(N_HEADS // 2, scores, finish, WA_LOOKAHEAD)


def _attention_specs(q_rows, dq, nkv, bsz, seq, n_ctx):
    steps = seq // q_rows
    ctx_blk0 = bsz * seq // n_ctx
    q_spec = pl.BlockSpec((q_rows, dq), lambda b, r: (b * steps + r, 0))
    kv_spec = pl.BlockSpec((seq, nkv), lambda b, r: (b, 0))
    ctx_spec = pl.BlockSpec((n_ctx, nkv), lambda b, r: (ctx_blk0 + b, 0))
    return (bsz, steps), q_spec, [kv_spec, kv_spec, ctx_spec, ctx_spec]


def _na_attention(q, k, v, t2, bsz, seq, n_ctx, name):
    dq = q.shape[1]
    grid, q_spec, kv_specs = _attention_specs(NA_ROWS_PER_STEP * GRID_W, dq, dq, bsz, seq, n_ctx)
    return pl.pallas_call(
        _na_body,
        grid=grid,
        in_specs=[q_spec, *kv_specs, _resident(t2.shape)],
        out_specs=q_spec,
        out_shape=jax.ShapeDtypeStruct((bsz * seq, dq), BF16),
        compiler_params=_cparams(2),
        name=name,
    )(q, k, v, k, v, t2)


def _wa_attention(sinks, q, k, v, bsz, seq, n_ctx, name):
    dq = q.shape[1]
    grid, q_spec, kv_specs = _attention_specs(WA_BLOCKS_PER_STEP * WA_WINDOW, dq, k.shape[1], bsz, seq, n_ctx)
    return pl.pallas_call(
        _wa_body,
        grid=grid,
        in_specs=[pl.BlockSpec(memory_space=pltpu.SMEM), q_spec, *kv_specs],
        out_specs=q_spec,
        out_shape=jax.ShapeDtypeStruct((bsz * seq, dq), BF16),
        compiler_params=_cparams(2),
        name=name,
    )(sinks, q, k, v, k, v)


def _na_bias_table(rpb):
    qcol = np.arange(GRID_W)[:, None]
    kcol = np.arange(GRID_W)[None, :]
    win_start = np.clip(qcol - NA_KW // 2, 0, GRID_W - NA_KW)
    valid = (kcol >= win_start) & (kcol < win_start + NA_KW)
    off = np.clip(kcol - qcol, -(NA_KW - 1), NA_KW - 1) + NA_KW - 1
    n_off = 2 * NA_KW - 1
    onehot = (off[None] == np.arange(n_off)[:, None, None]).astype(np.float32)
    pair = np.zeros((2 * n_off, GRID_W, 2 * GRID_W), np.float32)
    pair[:n_off, :, :GRID_W] = onehot
    pair[n_off:, :, GRID_W:] = onehot
    rpb2 = jnp.concatenate([rpb[:, :-1], rpb[:, 1:]], axis=-1).astype(F32)
    t = jnp.einsum("hdo,oqk->hdqk", rpb2, pair, precision=lax.Precision.HIGHEST)
    return jnp.where(np.concatenate([valid, valid], axis=1), t * LOG2E, NEG_INF)


def _rope_tables(seq, pad_rows):
    t = jnp.arange(seq, dtype=jnp.int32)
    n = HEAD_DIM // 4
    inv_freq = ROPE_BASE ** (-jnp.arange(n, dtype=F32) / n)
    ang_r = (t // GRID_W).astype(F32)[:, None] * inv_freq[None, :]
    ang_c = (t % GRID_W).astype(F32)[:, None] * inv_freq[None, :]
    ang = jnp.concatenate([ang_r, ang_r, ang_c, ang_c], axis=1)
    sign = np.tile(np.concatenate([-np.ones(n), np.ones(n)]), 2).astype(np.float32)
    cos = jnp.cos(ang)
    sin = jnp.sin(ang) * sign
    cos = jnp.concatenate([cos, jnp.ones((pad_rows, HEAD_DIM), F32)], axis=0)
    sin = jnp.concatenate([sin, jnp.zeros((pad_rows, HEAD_DIM), F32)], axis=0)
    return jnp.tile(cos, (1, LANES // HEAD_DIM)), jnp.tile(sin, (1, LANES // HEAD_DIM))


def kernel(x, c, ctx, c_ctx, w_mod, b_mod, ln_g, ln_b, ffn_w_in, ffn_w_out,
           na_w_qkv, na_w_o, na_rpb, wa_w_qkv, wa_w_o, wa_sinks):
    bsz, seq, d = x.shape
    n_ctx = ctx.shape[1]
    depth = w_mod.shape[0]
    d_ff = ffn_w_out.shape[2]
    alpha = (2 * depth) ** 0.25
    t_lat, t_ctx = bsz * seq, bsz * n_ctx
    t_all = t_lat + t_ctx
    nt_lat, nt_all = t_lat // TOKEN_TILE, t_all // TOKEN_TILE
    tiles_per_seq = seq // TOKEN_TILE
    dq = N_HEADS * HEAD_DIM

    mod_rows = 16
    cc = jnp.concatenate([c, c_ctx[None], jnp.zeros((mod_rows - bsz - 1, d), F32)], axis=0)
    mods = _modulation(cc, w_mod, b_mod)

    w_in, w_out = ffn_w_in[0, 0].astype(BF16), ffn_w_out[0, 0].astype(BF16)
    ffn_w_specs = [_resident((d, 2 * d_ff)), _resident((d_ff, d))]
    h_parts = [x.reshape(t_lat, d), ctx.reshape(t_ctx, d)]
    ctx_blk0 = t_lat // n_ctx

    for i in range(depth):
        last = i == depth - 1
        is_na = i % 2 == 0
        j = i // 2
        mod_spec = _mod_spec(i, nt_lat, tiles_per_seq, bsz, d)
        ln_spec = pl.BlockSpec((None, 3, d), lambda t, i=i: (i, 0, 0))
        mixer_qkv, mixer_o = (na_w_qkv, na_w_o) if is_na else (wa_w_qkv, wa_w_o)

        h_specs = _split_tok_specs(d, nt_lat) if len(h_parts) == 2 else [_tok_spec(d)]
        jobs = [_cast_job(ffn_w_in, (i, 1), CAST_ROWS, nt_all), _cast_job(ffn_w_out, (i, 1), CAST_ROWS_OUT, nt_all),
                _cast_job(mixer_qkv, (j,), CAST_ROWS, nt_all), _cast_job(mixer_o, (j,), CAST_ROWS, nt_all)]
        c_in, c_out, c_shape, c_args = zip(*jobs)
        h, w_in, w_out, w_qkv, w_o = pl.pallas_call(
            functools.partial(_ffn_body, alpha=alpha, n_x=len(h_parts), n_lat_tiles=nt_lat, n_cast=len(jobs)),
            grid=(nt_all,),
            in_specs=[mod_spec, *h_specs, *ffn_w_specs, ln_spec, ln_spec, *c_in],
            out_specs=[_tok_spec(d), *c_out],
            out_shape=[jax.ShapeDtypeStruct((t_all, d), F32), *c_shape],
            compiler_params=_cparams(1),
            name=f"ffn_pre_{i}",
        )(mods, *h_parts, w_in, w_out, ln_g, ln_b, *c_args)

        if is_na:
            q, k, v = pl.pallas_call(
                _qkv_na_body,
                grid=(nt_all,),
                in_specs=[mod_spec, _tok_spec(d), _resident(w_qkv.shape)],
                out_specs=[_tok_spec(dq)] * 3,
                out_shape=[jax.ShapeDtypeStruct((t_all, dq), BF16)] * 3,
                compiler_params=_cparams(1),
                name=f"qkv_na_{i}",
            )(mods, h, w_qkv)
            att = _na_attention(q, k, v, _na_bias_table(na_rpb[j]), bsz, seq, n_ctx, f"attn_na_{i}")
        else:
            nkv = WA_KV_HEADS * LANES
            cos, sin = _rope_tables(seq, TOKEN_TILE)
            rope_spec = pl.BlockSpec(
                (TOKEN_TILE, LANES), lambda t: (jnp.where(t < nt_lat, t % tiles_per_seq, tiles_per_seq), 0))
            q, k, v = pl.pallas_call(
                _qkv_wa_body,
                grid=(nt_all,),
                in_specs=[mod_spec, _tok_spec(d), rope_spec, rope_spec, _resident(w_qkv.shape)],
                out_specs=[_tok_spec(dq), _tok_spec(nkv), _tok_spec(nkv)],
                out_shape=[jax.ShapeDtypeStruct((t_all, dq), BF16),
                           jax.ShapeDtypeStruct((t_all, nkv), BF16),
                           jax.ShapeDtypeStruct((t_all, nkv), BF16)],
                compiler_params=_cparams(1),
                name=f"qkv_wa_{i}",
            )(mods, h, cos, sin, w_qkv)
            att = _wa_attention(wa_sinks[j], q, k, v, bsz, seq, n_ctx, f"attn_wa_{i}")

        att_parts = [att]
        if not last:
            ctx_spec = pl.BlockSpec((n_ctx, dq), lambda b: (ctx_blk0 + b, 0))
            att_parts.append(pl.pallas_call(
                _ctx_body,
                grid=(bsz,),
                in_specs=[ctx_spec] * 3,
                out_specs=pl.BlockSpec((n_ctx, dq), lambda b: (b, 0)),
                out_shape=jax.ShapeDtypeStruct((t_ctx, dq), BF16),
                compiler_params=_cparams(1),
                name=f"attn_ctx_{i}",
            )(q, k, v))

        n_tiles = nt_lat if last else nt_all
        att_specs = _split_tok_specs(dq, nt_lat) if len(att_parts) == 2 else [_tok_spec(dq)]
        jobs = [] if last else [_cast_job(ffn_w_in, (i + 1, 0), CAST_ROWS, n_tiles),
                                _cast_job(ffn_w_out, (i + 1, 0), CAST_ROWS_OUT, n_tiles)]
        c_in, c_out, c_shape, c_args = zip(*jobs) if jobs else ((), (), (), ())
        h, *next_w = pl.pallas_call(
            functools.partial(_post_body, alpha=alpha, n_a=len(att_parts), n_lat_tiles=nt_lat, n_cast=len(jobs)),
            grid=(n_tiles,),
            in_specs=[mod_spec, _tok_spec(d), *att_specs, _resident(w_o.shape), *ffn_w_specs,
                      ln_spec, ln_spec, *c_in],
            out_specs=[_tok_spec(d), *c_out],
            out_shape=[jax.ShapeDtypeStruct((n_tiles * TOKEN_TILE, d), F32), *c_shape],
            compiler_params=_cparams(1),
            name=f"post_{i}",
        )(mods, h, *att_parts, w_o, w_in, w_out, ln_g, ln_b, *c_args)
        if next_w:
            w_in, w_out = next_w
        h_parts = [h]

    return h.reshape(bsz, seq, d)
```

```python
import functools

import numpy as np
import jax
import jax.numpy as jnp
from jax import lax
from jax.experimental import pallas as pl
from jax.experimental.pallas import tpu as pltpu

F32 = jnp.float32
BF16 = jnp.bfloat16

GRID_W = 64
N_HEADS = 16
HEAD_DIM = 64
NA_KH = 8
NA_KW = 16
WA_KV_HEADS = 4
WA_WINDOW = 128
FFN_RES = 0.5
ROPE_BASE = 10000.0
N_MOD = 9
LN_EPS = 1e-5
NEG_INF = -1e30

LANES = 128
TOKEN_TILE = 512
MOD_COLS_PER_STEP = 3
CAST_ROWS = 32
CAST_ROWS_OUT = 176
FIRST_ROW_CHUNK = 128
ROW_CHUNK = 256
NA_ROWS_PER_STEP = 8
WA_BLOCKS_PER_STEP = 4
NA_LOOKAHEAD = 2
WA_LOOKAHEAD = 4
LOG2E = 1.4426950408889634
Q_SCALE = HEAD_DIM ** -0.5 * LOG2E
VMEM_LIMIT = 56 * 1024 * 1024


def _cparams(n_grid):
    return pltpu.CompilerParams(
        dimension_semantics=("arbitrary",) * n_grid, vmem_limit_bytes=VMEM_LIMIT)


def _resident(shape, lead=()):
    lead = tuple(lead)
    nd = len(shape)
    return pl.BlockSpec((None,) * len(lead) + tuple(shape), lambda *_: lead + (0,) * nd,
                        pipeline_mode=pl.Buffered(1))


def _tokens(refs, n_lat_tiles, rows):
    if len(refs) == 1:
        return refs[0][rows, :]
    return jnp.where(pl.program_id(0) < n_lat_tiles, refs[0][rows, :], refs[1][rows, :])


def _row_chunks(n):
    bounds = [0] + list(range(FIRST_ROW_CHUNK, n, ROW_CHUNK)) + [n]
    return [slice(a, b) for a, b in zip(bounds[:-1], bounds[1:])]


def _layer_norm(z, g, b):
    mu = jnp.mean(z, axis=-1, keepdims=True)
    var = jnp.mean(z * z, axis=-1, keepdims=True) - mu * mu
    return (z - mu) * lax.rsqrt(var + LN_EPS) * g + b


def _swiglu(u_bf16, win_ref, wout_ref):
    d_ff = wout_ref.shape[0]
    a = jnp.dot(u_bf16, win_ref[:, :d_ff], preferred_element_type=F32)
    v = jnp.dot(u_bf16, win_ref[:, d_ff:], preferred_element_type=F32)
    g = (a * jax.nn.sigmoid(a) * v).astype(BF16)
    return jnp.dot(g, wout_ref[...], preferred_element_type=F32)


def _ffn_residual(x, mod, base, win_ref, wout_ref, alpha):
    shift, scale, gate = mod[base:base + 1], mod[base + 1:base + 2], mod[base + 2:base + 3]
    u = (x * (1.0 + scale) + shift).astype(BF16)
    y = _swiglu(u, win_ref, wout_ref)
    return alpha * x + (FFN_RES * gate) * y


def _mod_body(c_ref, w_ref, b_ref, o_ref):
    c = c_ref[...]
    s = (c * jax.nn.sigmoid(c)).astype(BF16)
    o_ref[0] = jnp.dot(s, w_ref[0].astype(BF16), preferred_element_type=F32) + b_ref[0]


def _modulation(cc, w_mod, b_mod):
    depth, d, nd = w_mod.shape
    r = cc.shape[0]
    cols = MOD_COLS_PER_STEP * d
    out = pl.pallas_call(
        _mod_body,
        grid=(depth, nd // cols),
        in_specs=[
            pl.BlockSpec((r, d), lambda i, j: (0, 0)),
            pl.BlockSpec((1, d, cols), lambda i, j: (i, 0, j)),
            pl.BlockSpec((1, 1, cols), lambda i, j: (i, 0, j)),
        ],
        out_specs=pl.BlockSpec((1, r, cols), lambda i, j: (i, 0, j)),
        out_shape=jax.ShapeDtypeStruct((depth, r, nd), F32),
        compiler_params=_cparams(2),
        name="modulation",
    )(cc, w_mod, b_mod.reshape(depth, 1, nd))
    return out.reshape(depth, r, nd // d, d)


def _split_cast_refs(refs, n_cast):
    if not n_cast:
        return refs, ()
    main = refs[:-(2 * n_cast + 1)] + (refs[-(n_cast + 1)],)
    return main, tuple(zip(refs[-(2 * n_cast + 1):-(n_cast + 1)], refs[-n_cast:]))


def _cast_weights(cast_refs):
    for src, dst in cast_refs:
        dst[...] = src[...].astype(BF16)


def _ffn_body(mod_ref, *refs, alpha, n_x, n_lat_tiles, n_cast):
    refs, cast_refs = _split_cast_refs(refs, n_cast)
    x_refs, (win_ref, wout_ref, g_ref, b_ref, o_ref) = refs[:n_x], refs[n_x:]
    _cast_weights(cast_refs)
    for rows in _row_chunks(o_ref.shape[0]):
        z = _ffn_residual(_tokens(x_refs, n_lat_tiles, rows), mod_ref[0], 0, win_ref, wout_ref, alpha)
        o_ref[rows, :] = _layer_norm(z, g_ref[0:1], b_ref[0:1])


def _post_body(mod_ref, h_ref, *refs, alpha, n_a, n_lat_tiles, n_cast):
    refs, cast_refs = _split_cast_refs(refs, n_cast)
    a_refs, (wo_ref, win_ref, wout_ref, g_ref, b_ref, o_ref) = refs[:n_a], refs[n_a:]
    _cast_weights(cast_refs)

    chunks = _row_chunks(o_ref.shape[0])
    ys = [jnp.dot(_tokens(a_refs, n_lat_tiles, rows), wo_ref[...], preferred_element_type=F32)
          for rows in chunks]
    for rows, y in zip(chunks, ys):
        h = _layer_norm(alpha * h_ref[rows, :] + mod_ref[0][5:6] * y, g_ref[1:2], b_ref[1:2])
        z = _ffn_residual(h, mod_ref[0], 6, win_ref, wout_ref, alpha)
        o_ref[rows, :] = _layer_norm(z, g_ref[2:3], b_ref[2:3])


def _qkv_na_body(mod_ref, h_ref, w_ref, q_ref, k_ref, v_ref):
    mod = mod_ref[0]
    d = h_ref.shape[1]
    for rows in _row_chunks(h_ref.shape[0]):
        u = (h_ref[rows, :] * (1.0 + mod[4:5]) + mod[3:4]).astype(BF16)
        r = jnp.dot(u, w_ref[...], preferred_element_type=F32)
        q_ref[rows, :] = (r[:, :d] * Q_SCALE).astype(BF16)
        k_ref[rows, :] = r[:, d:2 * d].astype(BF16)
        v_ref[rows, :] = r[:, 2 * d:].astype(BF16)


def _rope_lanes(x, cos, sin_signed, first_half):
    partner = jnp.where(first_half, pltpu.roll(x, LANES - 16, 1), pltpu.roll(x, 16, 1))
    return x * cos + partner * sin_signed


def _qkv_wa_body(mod_ref, h_ref, cos_ref, sin_ref, w_ref, q_ref, k_ref, v_ref):
    mod = mod_ref[0]
    d = h_ref.shape[1]
    nk = k_ref.shape[1]
    n_src = nk // (2 * LANES)
    for rows in _row_chunks(h_ref.shape[0]):
        u = (h_ref[rows, :] * (1.0 + mod[4:5]) + mod[3:4]).astype(BF16)
        r = jnp.dot(u, w_ref[...], preferred_element_type=F32)
        cos, sin = cos_ref[rows, :], sin_ref[rows, :]
        first_half = lax.broadcasted_iota(jnp.int32, cos.shape, 1) % 32 < 16
        for j in range(d // LANES):
            sl = slice(j * LANES, (j + 1) * LANES)
            q_ref[rows, sl] = (_rope_lanes(r[:, sl], cos, sin, first_half) * Q_SCALE).astype(BF16)
        lo = lax.broadcasted_iota(jnp.int32, cos.shape, 1) < HEAD_DIM
        for dst_ref, base, rope in ((k_ref, d, True), (v_ref, d + n_src * LANES, False)):
            for j in range(n_src):
                x = r[:, base + j * LANES:base + (j + 1) * LANES]
                if rope:
                    x = _rope_lanes(x, cos, sin, first_half)
                swapped = pltpu.roll(x, HEAD_DIM, 1)
                dst_ref[rows, 2 * j * LANES:(2 * j + 1) * LANES] = jnp.where(lo, x, swapped).astype(BF16)
                dst_ref[rows, (2 * j + 1) * LANES:(2 * j + 2) * LANES] = jnp.where(lo, swapped, x).astype(BF16)


def _mod_spec(layer, n_lat_tiles, tiles_per_seq, ctx_row, d):
    return pl.BlockSpec(
        (None, 1, N_MOD, d),
        lambda t: (layer, jnp.where(t < n_lat_tiles, t // tiles_per_seq, ctx_row), 0, 0))


def _tok_spec(width):
    return pl.BlockSpec((TOKEN_TILE, width), lambda t: (t, 0))


def _cast_job(weight, lead, rows_per_step, n_steps):
    lead = tuple(lead)
    rows, cols = weight.shape[len(lead):]
    n = rows // rows_per_step
    assert rows == n * rows_per_step and n_steps >= n
    in_spec = pl.BlockSpec((None,) * len(lead) + (rows_per_step, cols),
                           lambda t: lead + (jnp.minimum(t, n - 1), 0))
    out_spec = pl.BlockSpec((rows_per_step, cols), lambda t: (jnp.minimum(t, n - 1), 0))
    return in_spec, out_spec, jax.ShapeDtypeStruct((rows, cols), BF16), weight


def _split_tok_specs(width, n_lat_tiles):
    return [pl.BlockSpec((TOKEN_TILE, width), lambda t: (jnp.minimum(t, n_lat_tiles - 1), 0)),
            pl.BlockSpec((TOKEN_TILE, width), lambda t: (jnp.maximum(t - n_lat_tiles, 0), 0))]


def _split_heads(x):
    lo = lax.broadcasted_iota(jnp.int32, x.shape, 1) < HEAD_DIM
    zero = jnp.zeros_like(x)
    return jnp.concatenate([jnp.where(lo, x, zero), jnp.where(lo, zero, x)], axis=0)


def _merge_heads(o):
    n = o.shape[0] // 2
    lo = lax.broadcasted_iota(jnp.int32, (n, LANES), 1) < HEAD_DIM
    return jnp.where(lo, o[:n], o[n:])


def _qk(q, k):
    return lax.dot_general(q, k, (((1,), (1,)), ((), ())), preferred_element_type=F32)


def _lane_fold(op, *xs):
    parts = [x[:, i:i + LANES] for x in xs for i in range(0, x.shape[1], LANES)]
    return functools.reduce(op, parts)


def _softmax_numerators(s_win, s_ctx, sink):
    x_win, x_ctx = s_win.astype(BF16), s_ctx.astype(BF16)
    m = jnp.max(_lane_fold(jnp.maximum, x_win, x_ctx).astype(F32), axis=-1, keepdims=True)
    if sink is not None:
        m = jnp.maximum(m, sink)
    m_bf16 = m.astype(BF16)
    return jnp.exp2(x_win - m_bf16), jnp.exp2(x_ctx - m_bf16), m_bf16.astype(F32)


def _with_ones(v):
    return jnp.concatenate([v, jnp.ones_like(v)], axis=1)


def _normalise(o_sum, extra):
    l = o_sum[:, LANES:LANES + 1]
    if extra is not None:
        l = l + extra
    return o_sum[:, :LANES] / l


def _pipelined(n, scores, finish, lookahead):
    pending = [scores(i) for i in range(min(lookahead, n))]
    for i in range(n):
        if i + lookahead < n:
            pending.append(scores(i + lookahead))
        finish(i, *pending.pop(0))


def _na_body(q_ref, k_ref, v_ref, kc_ref, vc_ref, t2_ref, o_ref):
    rows = k_ref.shape[0] // GRID_W
    n_win = NA_KH * GRID_W
    n_rows = q_ref.shape[0] // GRID_W
    r0 = pl.program_id(1) * n_rows
    key_rows = [jnp.clip(r0 + i - NA_KH // 2, 0, rows - NA_KH) for i in range(n_rows)]
    k_offs = [pl.multiple_of(rs * GRID_W, GRID_W) for rs in key_rows]
    shifts = [rs - (r0 + i) + NA_KH - 1 for i, rs in enumerate(key_rows)]
    m_row = 2 * GRID_W

    def scores(p):
        sl = slice(p * LANES, (p + 1) * LANES)
        qs = [_split_heads(q_ref[i * GRID_W:(i + 1) * GRID_W, sl]) for i in range(n_rows)]
        s_win = jnp.concatenate(
            [_qk(qs[i], k_ref[pl.ds(k_offs[i], n_win), sl]) for i in range(n_rows)], axis=0)
        return s_win, _qk(jnp.concatenate(qs, axis=0), kc_ref[:, sl])

    def finish(p, s_win, s_ctx):
        sl = slice(p * LANES, (p + 1) * LANES)
        bias = jnp.concatenate(
            [jnp.concatenate(
                [jnp.concatenate([t2_ref[2 * p, sh + 2 * jj], t2_ref[2 * p + 1, sh + 2 * jj]], axis=0)
                 for jj in range(NA_KH // 2)], axis=1) for sh in shifts], axis=0)
        p_win, p_ctx, _ = _softmax_numerators(s_win + bias, s_ctx, None)
        o_win = jnp.concatenate(
            [jnp.dot(p_win[i * m_row:(i + 1) * m_row], _with_ones(v_ref[pl.ds(k_offs[i], n_win), sl]),
                     preferred_element_type=F32) for i in range(n_rows)], axis=0)
        o = _normalise(o_win + jnp.dot(p_ctx, _with_ones(vc_ref[:, sl]), preferred_element_type=F32), None)
        for i in range(n_rows):
            o_ref[i * GRID_W:(i + 1) * GRID_W, sl] = _merge_heads(o[i * m_row:(i + 1) * m_row]).astype(BF16)

    _pipelined(N_HEADS // 2, scores, finish, NA_LOOKAHEAD)


def _ctx_body(q_ref, k_ref, v_ref, o_ref):
    for p in range(N_HEADS // 2):
        sl = slice(p * LANES, (p + 1) * LANES)
        qs = _split_heads(q_ref[:, sl])
        s = _qk(qs, k_ref[:, sl])
        m = jnp.max(s, axis=-1, keepdims=True)
        e = jnp.exp2(s - m)
        l = jnp.sum(e, axis=-1, keepdims=True)
        o = jnp.dot(e.astype(BF16), v_ref[:, sl], preferred_element_type=F32)
        o_ref[:, sl] = _merge_heads(o / l).astype(BF16)


def _wa_body(sink_ref, q_ref, k_ref, v_ref, kc_ref, vc_ref, o_ref):
    seq = k_ref.shape[0]
    blk = WA_WINDOW
    n_blk = q_ref.shape[0] // blk
    n_win = 3 * blk
    qb0 = pl.program_id(1) * n_blk
    starts = [pl.multiple_of(jnp.clip((qb0 + b - 1) * blk, 0, seq - n_win), blk) for b in range(n_blk)]
    d_row = lax.broadcasted_iota(jnp.int32, (blk, n_win), 0) - lax.broadcasted_iota(jnp.int32, (blk, n_win), 1)
    bands = []
    for b in range(n_blk):
        dist = d_row + ((qb0 + b) * blk - starts[b])
        band = jnp.where(jnp.abs(dist) <= WA_WINDOW, 0.0, NEG_INF).astype(F32)
        bands += [band, band]
    band = jnp.concatenate(bands, axis=0)
    pairs_per_kv = N_HEADS // WA_KV_HEADS // 2
    m_blk = 2 * blk

    def scores(p):
        kv = slice(p // pairs_per_kv * LANES, (p // pairs_per_kv + 1) * LANES)
        qs = [_split_heads(q_ref[b * blk:(b + 1) * blk, p * LANES:(p + 1) * LANES]) for b in range(n_blk)]
        s_win = jnp.concatenate(
            [_qk(qs[b], k_ref[pl.ds(starts[b], n_win), kv]) for b in range(n_blk)], axis=0)
        return s_win, _qk(jnp.concatenate(qs, axis=0), kc_ref[:, kv])

    def finish(p, s_win, s_ctx):
        kv = slice(p // pairs_per_kv * LANES, (p // pairs_per_kv + 1) * LANES)
        sink = jnp.concatenate(
            [jnp.full((blk, 1), sink_ref[2 * p + j] * LOG2E, F32) for j in range(2)] * n_blk, axis=0)
        p_win, p_ctx, m = _softmax_numerators(s_win + band, s_ctx, sink)
        o_win = jnp.concatenate(
            [jnp.dot(p_win[b * m_blk:(b + 1) * m_blk], _with_ones(v_ref[pl.ds(starts[b], n_win), kv]),
                     preferred_element_type=F32) for b in range(n_blk)], axis=0)
        o = _normalise(o_win + jnp.dot(p_ctx, _with_ones(vc_ref[:, kv]), preferred_element_type=F32),
                       jnp.exp2(sink - m))
        for b in range(n_blk):
            o_ref[b * blk:(b + 1) * blk, p * LANES:(p + 1) * LANES] = _merge_heads(
                o[b * m_blk:(b + 1) * m_blk]).astype(BF16)

    _pipelined(N_HEADS // 2, scores, finish, WA_LOOKAHEAD)


def _attention_specs(q_rows, dq, nkv, bsz, seq, n_ctx):
    steps = seq // q_rows
    ctx_blk0 = bsz * seq // n_ctx
    q_spec = pl.BlockSpec((q_rows, dq), lambda b, r: (b * steps + r, 0))
    kv_spec = pl.BlockSpec((seq, nkv), lambda b, r: (b, 0))
    ctx_spec = pl.BlockSpec((n_ctx, nkv), lambda b, r: (ctx_blk0 + b, 0))
    return (bsz, steps), q_spec, [kv_spec, kv_spec, ctx_spec, ctx_spec]


def _na_attention(q, k, v, t2, bsz, seq, n_ctx, name):
    dq = q.shape[1]
    grid, q_spec, kv_specs = _attention_specs(NA_ROWS_PER_STEP * GRID_W, dq, dq, bsz, seq, n_ctx)
    return pl.pallas_call(
        _na_body,
        grid=grid,
        in_specs=[q_spec, *kv_specs, _resident(t2.shape)],
        out_specs=q_spec,
        out_shape=jax.ShapeDtypeStruct((bsz * seq, dq), BF16),
        compiler_params=_cparams(2),
        name=name,
    )(q, k, v, k, v, t2)


def _wa_attention(sinks, q, k, v, bsz, seq, n_ctx, name):
    dq = q.shape[1]
    grid, q_spec, kv_specs = _attention_specs(WA_BLOCKS_PER_STEP * WA_WINDOW, dq, k.shape[1], bsz, seq, n_ctx)
    return pl.pallas_call(
        _wa_body,
        grid=grid,
        in_specs=[pl.BlockSpec(memory_space=pltpu.SMEM), q_spec, *kv_specs],
        out_specs=q_spec,
        out_shape=jax.ShapeDtypeStruct((bsz * seq, dq), BF16),
        compiler_params=_cparams(2),
        name=name,
    )(sinks, q, k, v, k, v)


def _na_bias_table(rpb):
    qcol = np.arange(GRID_W)[:, None]
    kcol = np.arange(GRID_W)[None, :]
    win_start = np.clip(qcol - NA_KW // 2, 0, GRID_W - NA_KW)
    valid = (kcol >= win_start) & (kcol < win_start + NA_KW)
    off = np.clip(kcol - qcol, -(NA_KW - 1), NA_KW - 1) + NA_KW - 1
    n_off = 2 * NA_KW - 1
    onehot = (off[None] == np.arange(n_off)[:, None, None]).astype(np.float32)
    pair = np.zeros((2 * n_off, GRID_W, 2 * GRID_W), np.float32)
    pair[:n_off, :, :GRID_W] = onehot
    pair[n_off:, :, GRID_W:] = onehot
    rpb2 = jnp.concatenate([rpb[:, :-1], rpb[:, 1:]], axis=-1).astype(F32)
    t = jnp.einsum("hdo,oqk->hdqk", rpb2, pair, precision=lax.Precision.HIGHEST)
    return jnp.where(np.concatenate([valid, valid], axis=1), t * LOG2E, NEG_INF)


def _rope_tables(seq, pad_rows):
    t = np.arange(seq)
    n = HEAD_DIM // 4
    inv_freq = ROPE_BASE ** (-np.arange(n, dtype=np.float64) / n)
    ang_r = (t // GRID_W)[:, None] * inv_freq[None, :]
    ang_c = (t % GRID_W)[:, None] * inv_freq[None, :]
    ang = np.concatenate([ang_r, ang_r, ang_c, ang_c], axis=1)
    sign = np.tile(np.concatenate([-np.ones(n), np.ones(n)]), 2)
    cos = np.concatenate([np.cos(ang), np.ones((pad_rows, HEAD_DIM))], axis=0)
    sin = np.concatenate([np.sin(ang) * sign, np.zeros((pad_rows, HEAD_DIM))], axis=0)
    reps = (1, LANES // HEAD_DIM)
    return jnp.asarray(np.tile(cos, reps), F32), jnp.asarray(np.tile(sin, reps), F32)


def kernel(x, c, ctx, c_ctx, w_mod, b_mod, ln_g, ln_b, ffn_w_in, ffn_w_out,
           na_w_qkv, na_w_o, na_rpb, wa_w_qkv, wa_w_o, wa_sinks):
    bsz, seq, d = x.shape
    n_ctx = ctx.shape[1]
    depth = w_mod.shape[0]
    d_ff = ffn_w_out.shape[2]
    alpha = (2 * depth) ** 0.25
    t_lat, t_ctx = bsz * seq, bsz * n_ctx
    t_all = t_lat + t_ctx
    nt_lat, nt_all = t_lat // TOKEN_TILE, t_all // TOKEN_TILE
    tiles_per_seq = seq // TOKEN_TILE
    dq = N_HEADS * HEAD_DIM

    mod_rows = 16
    cc = jnp.concatenate([c, c_ctx[None], jnp.zeros((mod_rows - bsz - 1, d), F32)], axis=0)
    mods = _modulation(cc, w_mod, b_mod)

    w_in, w_out = ffn_w_in[0, 0].astype(BF16), ffn_w_out[0, 0].astype(BF16)
    ffn_w_specs = [_resident((d, 2 * d_ff)), _resident((d_ff, d))]
    h_parts = [x.reshape(t_lat, d), ctx.reshape(t_ctx, d)]
    ctx_blk0 = t_lat // n_ctx

    for i in range(depth):
        last = i == depth - 1
        is_na = i % 2 == 0
        j = i // 2
        mod_spec = _mod_spec(i, nt_lat, tiles_per_seq, bsz, d)
        ln_spec = pl.BlockSpec((None, 3, d), lambda t, i=i: (i, 0, 0))
        mixer_qkv, mixer_o = (na_w_qkv, na_w_o) if is_na else (wa_w_qkv, wa_w_o)

        h_specs = _split_tok_specs(d, nt_lat) if len(h_parts) == 2 else [_tok_spec(d)]
        jobs = [_cast_job(ffn_w_in, (i, 1), CAST_ROWS, nt_all), _cast_job(ffn_w_out, (i, 1), CAST_ROWS_OUT, nt_all),
                _cast_job(mixer_qkv, (j,), CAST_ROWS, nt_all), _cast_job(mixer_o, (j,), CAST_ROWS, nt_all)]
        c_in, c_out, c_shape, c_args = zip(*jobs)
        h, w_in, w_out, w_qkv, w_o = pl.pallas_call(
            functools.partial(_ffn_body, alpha=alpha, n_x=len(h_parts), n_lat_tiles=nt_lat, n_cast=len(jobs)),
            grid=(nt_all,),
            in_specs=[mod_spec, *h_specs, *ffn_w_specs, ln_spec, ln_spec, *c_in],
            out_specs=[_tok_spec(d), *c_out],
            out_shape=[jax.ShapeDtypeStruct((t_all, d), F32), *c_shape],
            compiler_params=_cparams(1),
            name=f"ffn_pre_{i}",
        )(mods, *h_parts, w_in, w_out, ln_g, ln_b, *c_args)

        if is_na:
            q, k, v = pl.pallas_call(
                _qkv_na_body,
                grid=(nt_all,),
                in_specs=[mod_spec, _tok_spec(d), _resident(w_qkv.shape)],
                out_specs=[_tok_spec(dq)] * 3,
                out_shape=[jax.ShapeDtypeStruct((t_all, dq), BF16)] * 3,
                compiler_params=_cparams(1),
                name=f"qkv_na_{i}",
            )(mods, h, w_qkv)
            att = _na_attention(q, k, v, _na_bias_table(na_rpb[j]), bsz, seq, n_ctx, f"attn_na_{i}")
        else:
            nkv = WA_KV_HEADS * LANES
            cos, sin = _rope_tables(seq, TOKEN_TILE)
            rope_spec = pl.BlockSpec(
                (TOKEN_TILE, LANES), lambda t: (jnp.where(t < nt_lat, t % tiles_per_seq, tiles_per_seq), 0))
            q, k, v = pl.pallas_call(
                _qkv_wa_body,
                grid=(nt_all,),
                in_specs=[mod_spec, _tok_spec(d), rope_spec, rope_spec, _resident(w_qkv.shape)],
                out_specs=[_tok_spec(dq), _tok_spec(nkv), _tok_spec(nkv)],
                out_shape=[jax.ShapeDtypeStruct((t_all, dq), BF16),
                           jax.ShapeDtypeStruct((t_all, nkv), BF16),
                           jax.ShapeDtypeStruct((t_all, nkv), BF16)],
                compiler_params=_cparams(1),
                name=f"qkv_wa_{i}",
            )(mods, h, cos, sin, w_qkv)
            att = _wa_attention(wa_sinks[j], q, k, v, bsz, seq, n_ctx, f"attn_wa_{i}")

        att_parts = [att]
        if not last:
            ctx_spec = pl.BlockSpec((n_ctx, dq), lambda b: (ctx_blk0 + b, 0))
            att_parts.append(pl.pallas_call(
                _ctx_body,
                grid=(bsz,),
                in_specs=[ctx_spec] * 3,
                out_specs=pl.BlockSpec((n_ctx, dq), lambda b: (b, 0)),
                out_shape=jax.ShapeDtypeStruct((t_ctx, dq), BF16),
                compiler_params=_cparams(1),
                name=f"attn_ctx_{i}",
            )(q, k, v))

        n_tiles = nt_lat if last else nt_all
        att_specs = _split_tok_specs(dq, nt_lat) if len(att_parts) == 2 else [_tok_spec(dq)]
        jobs = [] if last else [_cast_job(ffn_w_in, (i + 1, 0), CAST_ROWS, n_tiles),
                                _cast_job(ffn_w_out, (i + 1, 0), CAST_ROWS_OUT, n_tiles)]
        c_in, c_out, c_shape, c_args = zip(*jobs) if jobs else ((), (), (), ())
        h, *next_w = pl.pallas_call(
            functools.partial(_post_body, alpha=alpha, n_a=len(att_parts), n_lat_tiles=nt_lat, n_cast=len(jobs)),
            grid=(n_tiles,),
            in_specs=[mod_spec, _tok_spec(d), *att_specs, _resident(w_o.shape), *ffn_w_specs,
                      ln_spec, ln_spec, *c_in],
            out_specs=[_tok_spec(d), *c_out],
            out_shape=[jax.ShapeDtypeStruct((n_tiles * TOKEN_TILE, d), F32), *c_shape],
            compiler_params=_cparams(1),
            name=f"post_{i}",
        )(mods, h, *att_parts, w_o, w_in, w_out, ln_g, ln_b, *c_args)
        if next_w:
            w_in, w_out = next_w
        h_parts = [h]

    return h.reshape(bsz, seq, d)
```

```python
import functools

import numpy as np
import jax
import jax.numpy as jnp
from jax import lax
from jax.experimental import pallas as pl
from jax.experimental.pallas import tpu as pltpu

F32 = jnp.float32
BF16 = jnp.bfloat16

GRID_W = 64
N_HEADS = 16
HEAD_DIM = 64
NA_KH = 8
NA_KW = 16
WA_KV_HEADS = 4
WA_WINDOW = 128
FFN_RES = 0.5
ROPE_BASE = 10000.0
N_MOD = 9
LN_EPS = 1e-5
NEG_INF = -1e30

LANES = 128
TOKEN_TILE = 512
MOD_COLS_PER_STEP = 3
CAST_ROWS = 32
CAST_ROWS_OUT = 176
FIRST_ROW_CHUNK = 128
ROW_CHUNK = 256
NA_ROWS_PER_STEP = 8
WA_BLOCKS_PER_STEP = 4
NA_LOOKAHEAD = 2
WA_LOOKAHEAD = 4
LOG2E = 1.4426950408889634
Q_SCALE = HEAD_DIM ** -0.5 * LOG2E
VMEM_LIMIT = 56 * 1024 * 1024


def _cparams(n_grid):
    return pltpu.CompilerParams(
        dimension_semantics=("arbitrary",) * n_grid, vmem_limit_bytes=VMEM_LIMIT)


def _resident(shape, lead=()):
    lead = tuple(lead)
    nd = len(shape)
    return pl.BlockSpec((None,) * len(lead) + tuple(shape), lambda *_: lead + (0,) * nd,
                        pipeline_mode=pl.Buffered(1))


def _tokens(refs, n_lat_tiles, rows):
    if len(refs) == 1:
        return refs[0][rows, :]
    return jnp.where(pl.program_id(0) < n_lat_tiles, refs[0][rows, :], refs[1][rows, :])


def _row_chunks(n):
    bounds = [0] + list(range(FIRST_ROW_CHUNK, n, ROW_CHUNK)) + [n]
    return [slice(a, b) for a, b in zip(bounds[:-1], bounds[1:])]


def _layer_norm(z, g, b):
    mu = jnp.mean(z, axis=-1, keepdims=True)
    var = jnp.mean(z * z, axis=-1, keepdims=True) - mu * mu
    return (z - mu) * lax.rsqrt(var + LN_EPS) * g + b


def _swiglu(u_bf16, win_ref, wout_ref):
    d_ff = wout_ref.shape[0]
    a = jnp.dot(u_bf16, win_ref[:, :d_ff], preferred_element_type=F32)
    v = jnp.dot(u_bf16, win_ref[:, d_ff:], preferred_element_type=F32)
    g = (a * jax.nn.sigmoid(a) * v).astype(BF16)
    return jnp.dot(g, wout_ref[...], preferred_element_type=F32)


def _ffn_residual(x, mod, base, win_ref, wout_ref, alpha):
    shift, scale, gate = mod[base:base + 1], mod[base + 1:base + 2], mod[base + 2:base + 3]
    u = (x * (1.0 + scale) + shift).astype(BF16)
    y = _swiglu(u, win_ref, wout_ref)
    return alpha * x + (FFN_RES * gate) * y


def _mod_body(c_ref, w_ref, b_ref, o_ref):
    c = c_ref[...]
    s = (c * jax.nn.sigmoid(c)).astype(BF16)
    o_ref[0] = jnp.dot(s, w_ref[0].astype(BF16), preferred_element_type=F32) + b_ref[0]


def _modulation(cc, w_mod, b_mod):
    depth, d, nd = w_mod.shape
    r = cc.shape[0]
    cols = MOD_COLS_PER_STEP * d
    out = pl.pallas_call(
        _mod_body,
        grid=(depth, nd // cols),
        in_specs=[
            pl.BlockSpec((r, d), lambda i, j: (0, 0)),
            pl.BlockSpec((1, d, cols), lambda i, j: (i, 0, j)),
            pl.BlockSpec((1, 1, cols), lambda i, j: (i, 0, j)),
        ],
        out_specs=pl.BlockSpec((1, r, cols), lambda i, j: (i, 0, j)),
        out_shape=jax.ShapeDtypeStruct((depth, r, nd), F32),
        compiler_params=_cparams(2),
        name="modulation",
    )(cc, w_mod, b_mod.reshape(depth, 1, nd))
    return out.reshape(depth, r, nd // d, d)


def _split_cast_refs(refs, n_cast):
    if not n_cast:
        return refs, ()
    main = refs[:-(2 * n_cast + 1)] + (refs[-(n_cast + 1)],)
    return main, tuple(zip(refs[-(2 * n_cast + 1):-(n_cast + 1)], refs[-n_cast:]))


def _cast_weights(cast_refs):
    for src, dst in cast_refs:
        dst[...] = src[...].astype(BF16)


def _ffn_body(mod_ref, *refs, alpha, n_x, n_lat_tiles, n_cast):
    refs, cast_refs = _split_cast_refs(refs, n_cast)
    x_refs, (win_ref, wout_ref, g_ref, b_ref, o_ref) = refs[:n_x], refs[n_x:]
    _cast_weights(cast_refs)
    for rows in _row_chunks(o_ref.shape[0]):
        z = _ffn_residual(_tokens(x_refs, n_lat_tiles, rows), mod_ref[0], 0, win_ref, wout_ref, alpha)
        o_ref[rows, :] = _layer_norm(z, g_ref[0:1], b_ref[0:1])


def _post_body(mod_ref, h_ref, *refs, alpha, n_a, n_lat_tiles, n_cast):
    refs, cast_refs = _split_cast_refs(refs, n_cast)
    a_refs, (wo_ref, win_ref, wout_ref, g_ref, b_ref, o_ref) = refs[:n_a], refs[n_a:]
    _cast_weights(cast_refs)

    chunks = _row_chunks(o_ref.shape[0])
    ys = [jnp.dot(_tokens(a_refs, n_lat_tiles, rows), wo_ref[...], preferred_element_type=F32)
          for rows in chunks]
    for rows, y in zip(chunks, ys):
        h = _layer_norm(alpha * h_ref[rows, :] + mod_ref[0][5:6] * y, g_ref[1:2], b_ref[1:2])
        z = _ffn_residual(h, mod_ref[0], 6, win_ref, wout_ref, alpha)
        o_ref[rows, :] = _layer_norm(z, g_ref[2:3], b_ref[2:3])


def _qkv_na_body(mod_ref, h_ref, w_ref, q_ref, k_ref, v_ref):
    mod = mod_ref[0]
    d = h_ref.shape[1]
    u = (h_ref[...] * (1.0 + mod[4:5]) + mod[3:4]).astype(BF16)
    r = jnp.dot(u, w_ref[...], preferred_element_type=F32)
    q_ref[...] = (r[:, :d] * Q_SCALE).astype(BF16)
    k_ref[...] = r[:, d:2 * d].astype(BF16)
    v_ref[...] = r[:, 2 * d:].astype(BF16)


def _rope_lanes(x, cos, sin_signed, first_half):
    partner = jnp.where(first_half, pltpu.roll(x, LANES - 16, 1), pltpu.roll(x, 16, 1))
    return x * cos + partner * sin_signed


def _qkv_wa_body(mod_ref, h_ref, cos_ref, sin_ref, w_ref, q_ref, k_ref, v_ref):
    mod = mod_ref[0]
    d = h_ref.shape[1]
    nk = k_ref.shape[1]
    n_src = nk // (2 * LANES)
    for rows in _row_chunks(h_ref.shape[0]):
        u = (h_ref[rows, :] * (1.0 + mod[4:5]) + mod[3:4]).astype(BF16)
        r = jnp.dot(u, w_ref[...], preferred_element_type=F32)
        cos, sin = cos_ref[rows, :], sin_ref[rows, :]
        first_half = lax.broadcasted_iota(jnp.int32, cos.shape, 1) % 32 < 16
        for j in range(d // LANES):
            sl = slice(j * LANES, (j + 1) * LANES)
            q_ref[rows, sl] = (_rope_lanes(r[:, sl], cos, sin, first_half) * Q_SCALE).astype(BF16)
        lo = lax.broadcasted_iota(jnp.int32, cos.shape, 1) < HEAD_DIM
        for dst_ref, base, rope in ((k_ref, d, True), (v_ref, d + n_src * LANES, False)):
            for j in range(n_src):
                x = r[:, base + j * LANES:base + (j + 1) * LANES]
                if rope:
                    x = _rope_lanes(x, cos, sin, first_half)
                swapped = pltpu.roll(x, HEAD_DIM, 1)
                dst_ref[rows, 2 * j * LANES:(2 * j + 1) * LANES] = jnp.where(lo, x, swapped).astype(BF16)
                dst_ref[rows, (2 * j + 1) * LANES:(2 * j + 2) * LANES] = jnp.where(lo, swapped, x).astype(BF16)


def _mod_spec(layer, n_lat_tiles, tiles_per_seq, ctx_row, d):
    return pl.BlockSpec(
        (None, 1, N_MOD, d),
        lambda t: (layer, jnp.where(t < n_lat_tiles, t // tiles_per_seq, ctx_row), 0, 0))


def _tok_spec(width):
    return pl.BlockSpec((TOKEN_TILE, width), lambda t: (t, 0))


def _cast_job(weight, lead, rows_per_step, n_steps):
    lead = tuple(lead)
    rows, cols = weight.shape[len(lead):]
    n = rows // rows_per_step
    assert rows == n * rows_per_step and n_steps >= n
    in_spec = pl.BlockSpec((None,) * len(lead) + (rows_per_step, cols),
                           lambda t: lead + (jnp.minimum(t, n - 1), 0))
    out_spec = pl.BlockSpec((rows_per_step, cols), lambda t: (jnp.minimum(t, n - 1), 0))
    return in_spec, out_spec, jax.ShapeDtypeStruct((rows, cols), BF16), weight


def _split_tok_specs(width, n_lat_tiles):
    return [pl.BlockSpec((TOKEN_TILE, width), lambda t: (jnp.minimum(t, n_lat_tiles - 1), 0)),
            pl.BlockSpec((TOKEN_TILE, width), lambda t: (jnp.maximum(t - n_lat_tiles, 0), 0))]


def _split_heads(x):
    lo = lax.broadcasted_iota(jnp.int32, x.shape, 1) < HEAD_DIM
    zero = jnp.zeros_like(x)
    return jnp.concatenate([jnp.where(lo, x, zero), jnp.where(lo, zero, x)], axis=0)


def _merge_heads(o):
    n = o.shape[0] // 2
    lo = lax.broadcasted_iota(jnp.int32, (n, LANES), 1) < HEAD_DIM
    return jnp.where(lo, o[:n], o[n:])


def _qk(q, k):
    return lax.dot_general(q, k, (((1,), (1,)), ((), ())), preferred_element_type=F32)


def _lane_fold(op, *xs):
    parts = [x[:, i:i + LANES] for x in xs for i in range(0, x.shape[1], LANES)]
    return functools.reduce(op, parts)


def _softmax_numerators(s_win, s_ctx, sink):
    x_win, x_ctx = s_win.astype(BF16), s_ctx.astype(BF16)
    m = jnp.max(_lane_fold(jnp.maximum, x_win, x_ctx).astype(F32), axis=-1, keepdims=True)
    if sink is not None:
        m = jnp.maximum(m, sink)
    m_bf16 = m.astype(BF16)
    return jnp.exp2(x_win - m_bf16), jnp.exp2(x_ctx - m_bf16), m_bf16.astype(F32)


def _with_ones(v):
    return jnp.concatenate([v, jnp.ones_like(v)], axis=1)


def _normalise(o_sum, extra):
    l = o_sum[:, LANES:LANES + 1]
    if extra is not None:
        l = l + extra
    return o_sum[:, :LANES] / l


def _pipelined(n, scores, finish, lookahead):
    pending = [scores(i) for i in range(min(lookahead, n))]
    for i in range(n):
        if i + lookahead < n:
            pending.append(scores(i + lookahead))
        finish(i, *pending.pop(0))


def _na_body(q_ref, k_ref, v_ref, kc_ref, vc_ref, t2_ref, o_ref):
    rows = k_ref.shape[0] // GRID_W
    n_win = NA_KH * GRID_W
    n_rows = q_ref.shape[0] // GRID_W
    r0 = pl.program_id(1) * n_rows
    key_rows = [jnp.clip(r0 + i - NA_KH // 2, 0, rows - NA_KH) for i in range(n_rows)]
    k_offs = [pl.multiple_of(rs * GRID_W, GRID_W) for rs in key_rows]
    shifts = [rs - (r0 + i) + NA_KH - 1 for i, rs in enumerate(key_rows)]
    m_row = 2 * GRID_W

    def scores(p):
        sl = slice(p * LANES, (p + 1) * LANES)
        qs = [_split_heads(q_ref[i * GRID_W:(i + 1) * GRID_W, sl]) for i in range(n_rows)]
        s_win = jnp.concatenate(
            [_qk(qs[i], k_ref[pl.ds(k_offs[i], n_win), sl]) for i in range(n_rows)], axis=0)
        return s_win, _qk(jnp.concatenate(qs, axis=0), kc_ref[:, sl])

    def finish(p, s_win, s_ctx):
        sl = slice(p * LANES, (p + 1) * LANES)
        bias = jnp.concatenate(
            [jnp.concatenate(
                [jnp.concatenate([t2_ref[2 * p, sh + 2 * jj], t2_ref[2 * p + 1, sh + 2 * jj]], axis=0)
                 for jj in range(NA_KH // 2)], axis=1) for sh in shifts], axis=0)
        p_win, p_ctx, _ = _softmax_numerators(s_win + bias, s_ctx, None)
        o_win = jnp.concatenate(
            [jnp.dot(p_win[i * m_row:(i + 1) * m_row], _with_ones(v_ref[pl.ds(k_offs[i], n_win), sl]),
                     preferred_element_type=F32) for i in range(n_rows)], axis=0)
        o = _normalise(o_win + jnp.dot(p_ctx, _with_ones(vc_ref[:, sl]), preferred_element_type=F32), None)
        for i in range(n_rows):
            o_ref[i * GRID_W:(i + 1) * GRID_W, sl] = _merge_heads(o[i * m_row:(i + 1) * m_row]).astype(BF16)

    _pipelined(N_HEADS // 2, scores, finish, NA_LOOKAHEAD)


def _ctx_body(q_ref, k_ref, v_ref, o_ref):
    for p in range(N_HEADS // 2):
        sl = slice(p * LANES, (p + 1) * LANES)
        qs = _split_heads(q_ref[:, sl])
        s = _qk(qs, k_ref[:, sl])
        m = jnp.max(s, axis=-1, keepdims=True)
        e = jnp.exp2(s - m)
        l = jnp.sum(e, axis=-1, keepdims=True)
        o = jnp.dot(e.astype(BF16), v_ref[:, sl], preferred_element_type=F32)
        o_ref[:, sl] = _merge_heads(o / l).astype(BF16)


def _wa_body(sink_ref, q_ref, k_ref, v_ref, kc_ref, vc_ref, o_ref):
    seq = k_ref.shape[0]
    blk = WA_WINDOW
    n_blk = q_ref.shape[0] // blk
    n_win = 3 * blk
    qb0 = pl.program_id(1) * n_blk
    starts = [pl.multiple_of(jnp.clip((qb0 + b - 1) * blk, 0, seq - n_win), blk) for b in range(n_blk)]
    d_row = lax.broadcasted_iota(jnp.int32, (blk, n_win), 0) - lax.broadcasted_iota(jnp.int32, (blk, n_win), 1)
    bands = []
    for b in range(n_blk):
        dist = d_row + ((qb0 + b) * blk - starts[b])
        band = jnp.where(jnp.abs(dist) <= WA_WINDOW, 0.0, NEG_INF).astype(F32)
        bands += [band, band]
    band = jnp.concatenate(bands, axis=0)
    pairs_per_kv = N_HEADS // WA_KV_HEADS // 2
    m_blk = 2 * blk

    def scores(p):
        kv = slice(p // pairs_per_kv * LANES, (p // pairs_per_kv + 1) * LANES)
        qs = [_split_heads(q_ref[b * blk:(b + 1) * blk, p * LANES:(p + 1) * LANES]) for b in range(n_blk)]
        s_win = jnp.concatenate(
            [_qk(qs[b], k_ref[pl.ds(starts[b], n_win), kv]) for b in range(n_blk)], axis=0)
        return s_win, _qk(jnp.concatenate(qs, axis=0), kc_ref[:, kv])

    def finish(p, s_win, s_ctx):
        kv = slice(p // pairs_per_kv * LANES, (p // pairs_per_kv + 1) * LANES)
        sink = jnp.concatenate(
            [jnp.full((blk, 1), sink_ref[2 * p + j] * LOG2E, F32) for j in range(2)] * n_blk, axis=0)
        p_win, p_ctx, m = _softmax_numerators(s_win + band, s_ctx, sink)
        o_win = jnp.concatenate(
            [jnp.dot(p_win[b * m_blk:(b + 1) * m_blk], _with_ones(v_ref[pl.ds(starts[b], n_win), kv]),
                     preferred_element_type=F32) for b in range(n_blk)], axis=0)
        o = _normalise(o_win + jnp.dot(p_ctx, _with_ones(vc_ref[:, kv]), preferred_element_type=F32),
                       jnp.exp2(sink - m))
        for b in range(n_blk):
            o_ref[b * blk:(b + 1) * blk, p * LANES:(p + 1) * LANES] = _merge_heads(
                o[b * m_blk:(b + 1) * m_blk]).astype(BF16)

    _pipelined(N_HEADS // 2, scores, finish, WA_LOOKAHEAD)


def _attention_specs(q_rows, dq, nkv, bsz, seq, n_ctx):
    steps = seq // q_rows
    ctx_blk0 = bsz * seq // n_ctx
    q_spec = pl.BlockSpec((q_rows, dq), lambda b, r: (b * steps + r, 0))
    kv_spec = pl.BlockSpec((seq, nkv), lambda b, r: (b, 0))
    ctx_spec = pl.BlockSpec((n_ctx, nkv), lambda b, r: (ctx_blk0 + b, 0))
    return (bsz, steps), q_spec, [kv_spec, kv_spec, ctx_spec, ctx_spec]


def _na_attention(q, k, v, t2, bsz, seq, n_ctx, name):
    dq = q.shape[1]
    grid, q_spec, kv_specs = _attention_specs(NA_ROWS_PER_STEP * GRID_W, dq, dq, bsz, seq, n_ctx)
    return pl.pallas_call(
        _na_body,
        grid=grid,
        in_specs=[q_spec, *kv_specs, _resident(t2.shape)],
        out_specs=q_spec,
        out_shape=jax.ShapeDtypeStruct((bsz * seq, dq), BF16),
        compiler_params=_cparams(2),
        name=name,
    )(q, k, v, k, v, t2)


def _wa_attention(sinks, q, k, v, bsz, seq, n_ctx, name):
    dq = q.shape[1]
    grid, q_spec, kv_specs = _attention_specs(WA_BLOCKS_PER_STEP * WA_WINDOW, dq, k.shape[1], bsz, seq, n_ctx)
    return pl.pallas_call(
        _wa_body,
        grid=grid,
        in_specs=[pl.BlockSpec(memory_space=pltpu.SMEM), q_spec, *kv_specs],
        out_specs=q_spec,
        out_shape=jax.ShapeDtypeStruct((bsz * seq, dq), BF16),
        compiler_params=_cparams(2),
        name=name,
    )(sinks, q, k, v, k, v)


def _na_bias_table(rpb):
    qcol = np.arange(GRID_W)[:, None]
    kcol = np.arange(GRID_W)[None, :]
    win_start = np.clip(qcol - NA_KW // 2, 0, GRID_W - NA_KW)
    valid = (kcol >= win_start) & (kcol < win_start + NA_KW)
    off = np.clip(kcol - qcol, -(NA_KW - 1), NA_KW - 1) + NA_KW - 1
    n_off = 2 * NA_KW - 1
    onehot = (off[None] == np.arange(n_off)[:, None, None]).astype(np.float32)
    pair = np.zeros((2 * n_off, GRID_W, 2 * GRID_W), np.float32)
    pair[:n_off, :, :GRID_W] = onehot
    pair[n_off:, :, GRID_W:] = onehot
    rpb2 = jnp.concatenate([rpb[:, :-1], rpb[:, 1:]], axis=-1).astype(F32)
    t = jnp.einsum("hdo,oqk->hdqk", rpb2, pair, precision=lax.Precision.HIGHEST)
    return jnp.where(np.concatenate([valid, valid], axis=1), t * LOG2E, NEG_INF)


def _rope_tables(seq, pad_rows):
    t = np.arange(seq)
    n = HEAD_DIM // 4
    inv_freq = ROPE_BASE ** (-np.arange(n, dtype=np.float64) / n)
    ang_r = (t // GRID_W)[:, None] * inv_freq[None, :]
    ang_c = (t % GRID_W)[:, None] * inv_freq[None, :]
    ang = np.concatenate([ang_r, ang_r, ang_c, ang_c], axis=1)
    sign = np.tile(np.concatenate([-np.ones(n), np.ones(n)]), 2)
    cos = np.concatenate([np.cos(ang), np.ones((pad_rows, HEAD_DIM))], axis=0)
    sin = np.concatenate([np.sin(ang) * sign, np.zeros((pad_rows, HEAD_DIM))], axis=0)
    reps = (1, LANES // HEAD_DIM)
    return jnp.asarray(np.tile(cos, reps), F32), jnp.asarray(np.tile(sin, reps), F32)


def kernel(x, c, ctx, c_ctx, w_mod, b_mod, ln_g, ln_b, ffn_w_in, ffn_w_out,
           na_w_qkv, na_w_o, na_rpb, wa_w_qkv, wa_w_o, wa_sinks):
    bsz, seq, d = x.shape
    n_ctx = ctx.shape[1]
    depth = w_mod.shape[0]
    d_ff = ffn_w_out.shape[2]
    alpha = (2 * depth) ** 0.25
    t_lat, t_ctx = bsz * seq, bsz * n_ctx
    t_all = t_lat + t_ctx
    nt_lat, nt_all = t_lat // TOKEN_TILE, t_all // TOKEN_TILE
    tiles_per_seq = seq // TOKEN_TILE
    dq = N_HEADS * HEAD_DIM

    mod_rows = 16
    cc = jnp.concatenate([c, c_ctx[None], jnp.zeros((mod_rows - bsz - 1, d), F32)], axis=0)
    mods = _modulation(cc, w_mod, b_mod)

    w_in, w_out = ffn_w_in[0, 0].astype(BF16), ffn_w_out[0, 0].astype(BF16)
    ffn_w_specs = [_resident((d, 2 * d_ff)), _resident((d_ff, d))]
    h_parts = [x.reshape(t_lat, d), ctx.reshape(t_ctx, d)]
    ctx_blk0 = t_lat // n_ctx

    for i in range(depth):
        last = i == depth - 1
        is_na = i % 2 == 0
        j = i // 2
        mod_spec = _mod_spec(i, nt_lat, tiles_per_seq, bsz, d)
        ln_spec = pl.BlockSpec((None, 3, d), lambda t, i=i: (i, 0, 0))
        mixer_qkv, mixer_o = (na_w_qkv, na_w_o) if is_na else (wa_w_qkv, wa_w_o)

        h_specs = _split_tok_specs(d, nt_lat) if len(h_parts) == 2 else [_tok_spec(d)]
        jobs = [_cast_job(ffn_w_in, (i, 1), CAST_ROWS, nt_all), _cast_job(ffn_w_out, (i, 1), CAST_ROWS_OUT, nt_all),
                _cast_job(mixer_qkv, (j,), CAST_ROWS, nt_all), _cast_job(mixer_o, (j,), CAST_ROWS, nt_all)]
        c_in, c_out, c_shape, c_args = zip(*jobs)
        h, w_in, w_out, w_qkv, w_o = pl.pallas_call(
            functools.partial(_ffn_body, alpha=alpha, n_x=len(h_parts), n_lat_tiles=nt_lat, n_cast=len(jobs)),
            grid=(nt_all,),
            in_specs=[mod_spec, *h_specs, *ffn_w_specs, ln_spec, ln_spec, *c_in],
            out_specs=[_tok_spec(d), *c_out],
            out_shape=[jax.ShapeDtypeStruct((t_all, d), F32), *c_shape],
            compiler_params=_cparams(1),
            name=f"ffn_pre_{i}",
        )(mods, *h_parts, w_in, w_out, ln_g, ln_b, *c_args)

        if is_na:
            q, k, v = pl.pallas_call(
                _qkv_na_body,
                grid=(nt_all,),
                in_specs=[mod_spec, _tok_spec(d), _resident(w_qkv.shape)],
                out_specs=[_tok_spec(dq)] * 3,
                out_shape=[jax.ShapeDtypeStruct((t_all, dq), BF16)] * 3,
                compiler_params=_cparams(1),
                name=f"qkv_na_{i}",
            )(mods, h, w_qkv)
            att = _na_attention(q, k, v, _na_bias_table(na_rpb[j]), bsz, seq, n_ctx, f"attn_na_{i}")
        else:
            nkv = WA_KV_HEADS * LANES
            cos, sin = _rope_tables(seq, TOKEN_TILE)
            rope_spec = pl.BlockSpec(
                (TOKEN_TILE, LANES), lambda t: (jnp.where(t < nt_lat, t % tiles_per_seq, tiles_per_seq), 0))
            q, k, v = pl.pallas_call(
                _qkv_wa_body,
                grid=(nt_all,),
                in_specs=[mod_spec, _tok_spec(d), rope_spec, rope_spec, _resident(w_qkv.shape)],
                out_specs=[_tok_spec(dq), _tok_spec(nkv), _tok_spec(nkv)],
                out_shape=[jax.ShapeDtypeStruct((t_all, dq), BF16),
                           jax.ShapeDtypeStruct((t_all, nkv), BF16),
                           jax.ShapeDtypeStruct((t_all, nkv), BF16)],
                compiler_params=_cparams(1),
                name=f"qkv_wa_{i}",
            )(mods, h, cos, sin, w_qkv)
            att = _wa_attention(wa_sinks[j], q, k, v, bsz, seq, n_ctx, f"attn_wa_{i}")

        att_parts = [att]
        if not last:
            ctx_spec = pl.BlockSpec((n_ctx, dq), lambda b: (ctx_blk0 + b, 0))
            att_parts.append(pl.pallas_call(
                _ctx_body,
                grid=(bsz,),
                in_specs=[ctx_spec] * 3,
                out_specs=pl.BlockSpec((n_ctx, dq), lambda b: (b, 0)),
                out_shape=jax.ShapeDtypeStruct((t_ctx, dq), BF16),
                compiler_params=_cparams(1),
                name=f"attn_ctx_{i}",
            )(q, k, v))

        n_tiles = nt_lat if last else nt_all
        att_specs = _split_tok_specs(dq, nt_lat) if len(att_parts) == 2 else [_tok_spec(dq)]
        jobs = [] if last else [_cast_job(ffn_w_in, (i + 1, 0), CAST_ROWS, n_tiles),
                                _cast_job(ffn_w_out, (i + 1, 0), CAST_ROWS_OUT, n_tiles)]
        c_in, c_out, c_shape, c_args = zip(*jobs) if jobs else ((), (), (), ())
        h, *next_w = pl.pallas_call(
            functools.partial(_post_body, alpha=alpha, n_a=len(att_parts), n_lat_tiles=nt_lat, n_cast=len(jobs)),
            grid=(n_tiles,),
            in_specs=[mod_spec, _tok_spec(d), *att_specs, _resident(w_o.shape), *ffn_w_specs,
                      ln_spec, ln_spec, *c_in],
            out_specs=[_tok_spec(d), *c_out],
            out_shape=[jax.ShapeDtypeStruct((n_tiles * TOKEN_TILE, d), F32), *c_shape],
            compiler_params=_cparams(1),
            name=f"post_{i}",
        )(mods, h, *att_parts, w_o, w_in, w_out, ln_g, ln_b, *c_args)
        if next_w:
            w_in, w_out = next_w
        h_parts = [h]

    return h.reshape(bsz, seq, d)
```

```python
import functools

import numpy as np
import jax
import jax.numpy as jnp
from jax import lax
from jax.experimental import pallas as pl
from jax.experimental.pallas import tpu as pltpu

F32 = jnp.float32
BF16 = jnp.bfloat16

GRID_W = 64
N_HEADS = 16
HEAD_DIM = 64
NA_KH = 8
NA_KW = 16
WA_KV_HEADS = 4
WA_WINDOW = 128
FFN_RES = 0.5
ROPE_BASE = 10000.0
N_MOD = 9
LN_EPS = 1e-5
NEG_INF = -1e30

LANES = 128
TOKEN_TILE = 512
MOD_COLS_PER_STEP = 3
CAST_ROWS = 32
CAST_ROWS_OUT = 176
FIRST_ROW_CHUNK = 128
ROW_CHUNK = 256
NA_ROWS_PER_STEP = 8
WA_BLOCKS_PER_STEP = 4
NA_LOOKAHEAD = 2
WA_LOOKAHEAD = 4
LOG2E = 1.4426950408889634
Q_SCALE = HEAD_DIM ** -0.5 * LOG2E
VMEM_LIMIT = 56 * 1024 * 1024


def _cparams(n_grid):
    return pltpu.CompilerParams(
        dimension_semantics=("arbitrary",) * n_grid, vmem_limit_bytes=VMEM_LIMIT)


def _resident(shape):
    nd = len(shape)
    return pl.BlockSpec(tuple(shape), lambda *_: (0,) * nd, pipeline_mode=pl.Buffered(1))


def _tokens(refs, n_lat_tiles, rows):
    if len(refs) == 1:
        return refs[0][rows, :]
    return jnp.where(pl.program_id(0) < n_lat_tiles, refs[0][rows, :], refs[1][rows, :])


def _row_chunks(n):
    bounds = [0] + list(range(FIRST_ROW_CHUNK, n, ROW_CHUNK)) + [n]
    return [slice(a, b) for a, b in zip(bounds[:-1], bounds[1:])]


def _layer_norm(z, g, b):
    mu = jnp.mean(z, axis=-1, keepdims=True)
    var = jnp.mean(z * z, axis=-1, keepdims=True) - mu * mu
    return (z - mu) * lax.rsqrt(var + LN_EPS) * g + b


def _swiglu(u_bf16, win_ref, wout_ref):
    d_ff = wout_ref.shape[0]
    a = jnp.dot(u_bf16, win_ref[:, :d_ff], preferred_element_type=F32)
    v = jnp.dot(u_bf16, win_ref[:, d_ff:], preferred_element_type=F32)
    g = (a * jax.nn.sigmoid(a) * v).astype(BF16)
    return jnp.dot(g, wout_ref[...], preferred_element_type=F32)


def _ffn_residual(x, mod, base, win_ref, wout_ref, alpha):
    shift, scale, gate = mod[base:base + 1], mod[base + 1:base + 2], mod[base + 2:base + 3]
    u = (x * (1.0 + scale) + shift).astype(BF16)
    y = _swiglu(u, win_ref, wout_ref)
    return alpha * x + (FFN_RES * gate) * y


def _mod_body(c_ref, w_ref, b_ref, o_ref):
    c = c_ref[...]
    s = (c * jax.nn.sigmoid(c)).astype(BF16)
    o_ref[0] = jnp.dot(s, w_ref[0].astype(BF16), preferred_element_type=F32) + b_ref[0]


def _modulation(cc, w_mod, b_mod):
    depth, d, nd = w_mod.shape
    r = cc.shape[0]
    cols = MOD_COLS_PER_STEP * d
    out = pl.pallas_call(
        _mod_body,
        grid=(depth, nd // cols),
        in_specs=[
            pl.BlockSpec((r, d), lambda i, j: (0, 0)),
            pl.BlockSpec((1, d, cols), lambda i, j: (i, 0, j)),
            pl.BlockSpec((1, 1, cols), lambda i, j: (i, 0, j)),
        ],
        out_specs=pl.BlockSpec((1, r, cols), lambda i, j: (i, 0, j)),
        out_shape=jax.ShapeDtypeStruct((depth, r, nd), F32),
        compiler_params=_cparams(2),
        name="modulation",
    )(cc, w_mod, b_mod.reshape(depth, 1, nd))
    return out.reshape(depth, r, nd // d, d)


def _split_cast_refs(refs, n_cast):
    if not n_cast:
        return refs, ()
    main = refs[:-(2 * n_cast + 1)] + (refs[-(n_cast + 1)],)
    return main, tuple(zip(refs[-(2 * n_cast + 1):-(n_cast + 1)], refs[-n_cast:]))


def _cast_weights(cast_refs):
    for src, dst in cast_refs:
        dst[...] = src[...].astype(BF16)


def _ffn_body(mod_ref, *refs, alpha, n_x, n_lat_tiles, n_cast):
    refs, cast_refs = _split_cast_refs(refs, n_cast)
    x_refs, (win_ref, wout_ref, g_ref, b_ref, o_ref) = refs[:n_x], refs[n_x:]
    _cast_weights(cast_refs)
    for rows in _row_chunks(o_ref.shape[0]):
        z = _ffn_residual(_tokens(x_refs, n_lat_tiles, rows), mod_ref[0], 0, win_ref, wout_ref, alpha)
        o_ref[rows, :] = _layer_norm(z, g_ref[0:1], b_ref[0:1])


def _post_body(mod_ref, h_ref, *refs, alpha, n_a, n_lat_tiles, n_cast):
    refs, cast_refs = _split_cast_refs(refs, n_cast)
    a_refs, (wo_ref, win_ref, wout_ref, g_ref, b_ref, o_ref) = refs[:n_a], refs[n_a:]
    _cast_weights(cast_refs)

    chunks = _row_chunks(o_ref.shape[0])
    ys = [jnp.dot(_tokens(a_refs, n_lat_tiles, rows), wo_ref[...], preferred_element_type=F32)
          for rows in chunks]
    for rows, y in zip(chunks, ys):
        h = _layer_norm(alpha * h_ref[rows, :] + mod_ref[0][5:6] * y, g_ref[1:2], b_ref[1:2])
        z = _ffn_residual(h, mod_ref[0], 6, win_ref, wout_ref, alpha)
        o_ref[rows, :] = _layer_norm(z, g_ref[2:3], b_ref[2:3])


def _qkv_na_body(mod_ref, h_ref, w_ref, q_ref, k_ref, v_ref):
    mod = mod_ref[0]
    d = h_ref.shape[1]
    u = (h_ref[...] * (1.0 + mod[4:5]) + mod[3:4]).astype(BF16)
    r = jnp.dot(u, w_ref[...], preferred_element_type=F32)
    q_ref[...] = (r[:, :d] * Q_SCALE).astype(BF16)
    k_ref[...] = r[:, d:2 * d].astype(BF16)
    v_ref[...] = r[:, 2 * d:].astype(BF16)


def _rope_lanes(x, cos, sin_signed, first_half):
    partner = jnp.where(first_half, pltpu.roll(x, LANES - 16, 1), pltpu.roll(x, 16, 1))
    return x * cos + partner * sin_signed


def _qkv_wa_body(mod_ref, h_ref, cos_ref, sin_ref, w_ref, q_ref, k_ref, v_ref):
    mod = mod_ref[0]
    d = h_ref.shape[1]
    nk = k_ref.shape[1]
    n_src = nk // (2 * LANES)
    for rows in _row_chunks(h_ref.shape[0]):
        u = (h_ref[rows, :] * (1.0 + mod[4:5]) + mod[3:4]).astype(BF16)
        r = jnp.dot(u, w_ref[...], preferred_element_type=F32)
        cos, sin = cos_ref[rows, :], sin_ref[rows, :]
        first_half = lax.broadcasted_iota(jnp.int32, cos.shape, 1) % 32 < 16
        for j in range(d // LANES):
            sl = slice(j * LANES, (j + 1) * LANES)
            q_ref[rows, sl] = (_rope_lanes(r[:, sl], cos, sin, first_half) * Q_SCALE).astype(BF16)
        lo = lax.broadcasted_iota(jnp.int32, cos.shape, 1) < HEAD_DIM
        for dst_ref, base, rope in ((k_ref, d, True), (v_ref, d + n_src * LANES, False)):
            for j in range(n_src):
                x = r[:, base + j * LANES:base + (j + 1) * LANES]
                if rope:
                    x = _rope_lanes(x, cos, sin, first_half)
                swapped = pltpu.roll(x, HEAD_DIM, 1)
                dst_ref[rows, 2 * j * LANES:(2 * j + 1) * LANES] = jnp.where(lo, x, swapped).astype(BF16)
                dst_ref[rows, (2 * j + 1) * LANES:(2 * j + 2) * LANES] = jnp.where(lo, swapped, x).astype(BF16)


def _mod_spec(layer, n_lat_tiles, tiles_per_seq, ctx_row, d):
    return pl.BlockSpec(
        (None, 1, N_MOD, d),
        lambda t: (layer, jnp.where(t < n_lat_tiles, t // tiles_per_seq, ctx_row), 0, 0))


def _tok_spec(width):
    return pl.BlockSpec((TOKEN_TILE, width), lambda t: (t, 0))


def _cast_job(weight, lead, rows_per_step, n_steps):
    lead = tuple(lead)
    rows, cols = weight.shape[len(lead):]
    n = rows // rows_per_step
    assert rows == n * rows_per_step and n_steps >= n
    in_spec = pl.BlockSpec((None,) * len(lead) + (rows_per_step, cols),
                           lambda t: lead + (jnp.minimum(t, n - 1), 0))
    out_spec = pl.BlockSpec((rows_per_step, cols), lambda t: (jnp.minimum(t, n - 1), 0))
    return in_spec, out_spec, jax.ShapeDtypeStruct((rows, cols), BF16), weight


def _split_tok_specs(width, n_lat_tiles):
    return [pl.BlockSpec((TOKEN_TILE, width), lambda t: (jnp.minimum(t, n_lat_tiles - 1), 0)),
            pl.BlockSpec((TOKEN_TILE, width), lambda t: (jnp.maximum(t - n_lat_tiles, 0), 0))]


def _split_heads(x):
    lo = lax.broadcasted_iota(jnp.int32, x.shape, 1) < HEAD_DIM
    zero = jnp.zeros_like(x)
    return jnp.concatenate([jnp.where(lo, x, zero), jnp.where(lo, zero, x)], axis=0)


def _merge_heads(o):
    n = o.shape[0] // 2
    lo = lax.broadcasted_iota(jnp.int32, (n, LANES), 1) < HEAD_DIM
    return jnp.where(lo, o[:n], o[n:])


def _qk(q, k):
    return lax.dot_general(q, k, (((1,), (1,)), ((), ())), preferred_element_type=F32)


def _lane_fold(op, *xs):
    parts = [x[:, i:i + LANES] for x in xs for i in range(0, x.shape[1], LANES)]
    return functools.reduce(op, parts)


def _softmax_numerators(s_win, s_ctx, sink):
    x_win, x_ctx = s_win.astype(BF16), s_ctx.astype(BF16)
    m = jnp.max(_lane_fold(jnp.maximum, x_win, x_ctx).astype(F32), axis=-1, keepdims=True)
    if sink is not None:
        m = jnp.maximum(m, sink)
    m_bf16 = m.astype(BF16)
    return jnp.exp2(x_win - m_bf16), jnp.exp2(x_ctx - m_bf16), m_bf16.astype(F32)


def _with_ones(v):
    return jnp.concatenate([v, jnp.ones_like(v)], axis=1)


def _normalise(o_sum, extra):
    l = o_sum[:, LANES:LANES + 1]
    if extra is not None:
        l = l + extra
    return o_sum[:, :LANES] / l


def _pipelined(n, scores, finish, lookahead):
    pending = [scores(i) for i in range(min(lookahead, n))]
    for i in range(n):
        if i + lookahead < n:
            pending.append(scores(i + lookahead))
        finish(i, *pending.pop(0))


def _na_body(q_ref, k_ref, v_ref, kc_ref, vc_ref, t2_ref, o_ref):
    rows = k_ref.shape[0] // GRID_W
    n_win = NA_KH * GRID_W
    n_rows = q_ref.shape[0] // GRID_W
    r0 = pl.program_id(1) * n_rows
    key_rows = [jnp.clip(r0 + i - NA_KH // 2, 0, rows - NA_KH) for i in range(n_rows)]
    k_offs = [pl.multiple_of(rs * GRID_W, GRID_W) for rs in key_rows]
    shifts = [rs - (r0 + i) + NA_KH - 1 for i, rs in enumerate(key_rows)]
    m_row = 2 * GRID_W

    def scores(p):
        sl = slice(p * LANES, (p + 1) * LANES)
        qs = [_split_heads(q_ref[i * GRID_W:(i + 1) * GRID_W, sl]) for i in range(n_rows)]
        s_win = jnp.concatenate(
            [_qk(qs[i], k_ref[pl.ds(k_offs[i], n_win), sl]) for i in range(n_rows)], axis=0)
        return s_win, _qk(jnp.concatenate(qs, axis=0), kc_ref[:, sl])

    def finish(p, s_win, s_ctx):
        sl = slice(p * LANES, (p + 1) * LANES)
        bias = jnp.concatenate(
            [jnp.concatenate(
                [jnp.concatenate([t2_ref[2 * p, sh + 2 * jj], t2_ref[2 * p + 1, sh + 2 * jj]], axis=0)
                 for jj in range(NA_KH // 2)], axis=1) for sh in shifts], axis=0)
        p_win, p_ctx, _ = _softmax_numerators(s_win + bias, s_ctx, None)
        o_win = jnp.concatenate(
            [jnp.dot(p_win[i * m_row:(i + 1) * m_row], _with_ones(v_ref[pl.ds(k_offs[i], n_win), sl]),
                     preferred_element_type=F32) for i in range(n_rows)], axis=0)
        o = _normalise(o_win + jnp.dot(p_ctx, _with_ones(vc_ref[:, sl]), preferred_element_type=F32), None)
        for i in range(n_rows):
            o_ref[i * GRID_W:(i + 1) * GRID_W, sl] = _merge_heads(o[i * m_row:(i + 1) * m_row]).astype(BF16)

    _pipelined(N_HEADS // 2, scores, finish, NA_LOOKAHEAD)


def _ctx_body(q_ref, k_ref, v_ref, o_ref):
    for p in range(N_HEADS // 2):
        sl = slice(p * LANES, (p + 1) * LANES)
        qs = _split_heads(q_ref[:, sl])
        s = _qk(qs, k_ref[:, sl])
        m = jnp.max(s, axis=-1, keepdims=True)
        e = jnp.exp2(s - m)
        l = jnp.sum(e, axis=-1, keepdims=True)
        o = jnp.dot(e.astype(BF16), v_ref[:, sl], preferred_element_type=F32)
        o_ref[:, sl] = _merge_heads(o / l).astype(BF16)


def _wa_body(sink_ref, q_ref, k_ref, v_ref, kc_ref, vc_ref, o_ref):
    seq = k_ref.shape[0]
    blk = WA_WINDOW
    n_blk = q_ref.shape[0] // blk
    n_win = 3 * blk
    qb0 = pl.program_id(1) * n_blk
    starts = [pl.multiple_of(jnp.clip((qb0 + b - 1) * blk, 0, seq - n_win), blk) for b in range(n_blk)]
    d_row = lax.broadcasted_iota(jnp.int32, (blk, n_win), 0) - lax.broadcasted_iota(jnp.int32, (blk, n_win), 1)
    bands = []
    for b in range(n_blk):
        dist = d_row + ((qb0 + b) * blk - starts[b])
        band = jnp.where(jnp.abs(dist) <= WA_WINDOW, 0.0, NEG_INF).astype(F32)
        bands += [band, band]
    band = jnp.concatenate(bands, axis=0)
    pairs_per_kv = N_HEADS // WA_KV_HEADS // 2
    m_blk = 2 * blk

    def scores(p):
        kv = slice(p // pairs_per_kv * LANES, (p // pairs_per_kv + 1) * LANES)
        qs = [_split_heads(q_ref[b * blk:(b + 1) * blk, p * LANES:(p + 1) * LANES]) for b in range(n_blk)]
        s_win = jnp.concatenate(
            [_qk(qs[b], k_ref[pl.ds(starts[b], n_win), kv]) for b in range(n_blk)], axis=0)
        return s_win, _qk(jnp.concatenate(qs, axis=0), kc_ref[:, kv])

    def finish(p, s_win, s_ctx):
        kv = slice(p // pairs_per_kv * LANES, (p // pairs_per_kv + 1) * LANES)
        sink = jnp.concatenate(
            [jnp.full((blk, 1), sink_ref[2 * p + j] * LOG2E, F32) for j in range(2)] * n_blk, axis=0)
        p_win, p_ctx, m = _softmax_numerators(s_win + band, s_ctx, sink)
        o_win = jnp.concatenate(
            [jnp.dot(p_win[b * m_blk:(b + 1) * m_blk], _with_ones(v_ref[pl.ds(starts[b], n_win), kv]),
                     preferred_element_type=F32) for b in range(n_blk)], axis=0)
        o = _normalise(o_win + jnp.dot(p_ctx, _with_ones(vc_ref[:, kv]), preferred_element_type=F32),
                       jnp.exp2(sink - m))
        for b in range(n_blk):
            o_ref[b * blk:(b + 1) * blk, p * LANES:(p + 1) * LANES] = _merge_heads(
                o[b * m_blk:(b + 1) * m_blk]).astype(BF16)

    _pipelined(N_HEADS // 2, scores, finish, WA_LOOKAHEAD)


def _attention_specs(q_rows, dq, nkv, bsz, seq, n_ctx):
    steps = seq // q_rows
    ctx_blk0 = bsz * seq // n_ctx
    q_spec = pl.BlockSpec((q_rows, dq), lambda b, r: (b * steps + r, 0))
    kv_spec = pl.BlockSpec((seq, nkv), lambda b, r: (b, 0))
    ctx_spec = pl.BlockSpec((n_ctx, nkv), lambda b, r: (ctx_blk0 + b, 0))
    return (bsz, steps), q_spec, [kv_spec, kv_spec, ctx_spec, ctx_spec]


def _na_attention(q, k, v, t2, bsz, seq, n_ctx, name):
    dq = q.shape[1]
    grid, q_spec, kv_specs = _attention_specs(NA_ROWS_PER_STEP * GRID_W, dq, dq, bsz, seq, n_ctx)
    return pl.pallas_call(
        _na_body,
        grid=grid,
        in_specs=[q_spec, *kv_specs, _resident(t2.shape)],
        out_specs=q_spec,
        out_shape=jax.ShapeDtypeStruct((bsz * seq, dq), BF16),
        compiler_params=_cparams(2),
        name=name,
    )(q, k, v, k, v, t2)


def _wa_attention(sinks, q, k, v, bsz, seq, n_ctx, name):
    dq = q.shape[1]
    grid, q_spec, kv_specs = _attention_specs(WA_BLOCKS_PER_STEP * WA_WINDOW, dq, k.shape[1], bsz, seq, n_ctx)
    return pl.pallas_call(
        _wa_body,
        grid=grid,
        in_specs=[pl.BlockSpec(memory_space=pltpu.SMEM), q_spec, *kv_specs],
        out_specs=q_spec,
        out_shape=jax.ShapeDtypeStruct((bsz * seq, dq), BF16),
        compiler_params=_cparams(2),
        name=name,
    )(sinks, q, k, v, k, v)


def _na_bias_table(rpb):
    qcol = np.arange(GRID_W)[:, None]
    kcol = np.arange(GRID_W)[None, :]
    win_start = np.clip(qcol - NA_KW // 2, 0, GRID_W - NA_KW)
    valid = (kcol >= win_start) & (kcol < win_start + NA_KW)
    off = np.clip(kcol - qcol, -(NA_KW - 1), NA_KW - 1) + NA_KW - 1
    n_off = 2 * NA_KW - 1
    onehot = (off[None] == np.arange(n_off)[:, None, None]).astype(np.float32)
    pair = np.zeros((2 * n_off, GRID_W, 2 * GRID_W), np.float32)
    pair[:n_off, :, :GRID_W] = onehot
    pair[n_off:, :, GRID_W:] = onehot
    rpb2 = jnp.concatenate([rpb[:, :-1], rpb[:, 1:]], axis=-1).astype(F32)
    t = jnp.einsum("hdo,oqk->hdqk", rpb2, pair, precision=lax.Precision.HIGHEST)
    return jnp.where(np.concatenate([valid, valid], axis=1), t * LOG2E, NEG_INF)


def _rope_tables(seq, pad_rows):
    t = np.arange(seq)
    n = HEAD_DIM // 4
    inv_freq = ROPE_BASE ** (-np.arange(n, dtype=np.float64) / n)
    ang_r = (t // GRID_W)[:, None] * inv_freq[None, :]
    ang_c = (t % GRID_W)[:, None] * inv_freq[None, :]
    ang = np.concatenate([ang_r, ang_r, ang_c, ang_c], axis=1)
    sign = np.tile(np.concatenate([-np.ones(n), np.ones(n)]), 2)
    cos = np.concatenate([np.cos(ang), np.ones((pad_rows, HEAD_DIM))], axis=0)
    sin = np.concatenate([np.sin(ang) * sign, np.zeros((pad_rows, HEAD_DIM))], axis=0)
    reps = (1, LANES // HEAD_DIM)
    return jnp.asarray(np.tile(cos, reps), F32), jnp.asarray(np.tile(sin, reps), F32)


def kernel(x, c, ctx, c_ctx, w_mod, b_mod, ln_g, ln_b, ffn_w_in, ffn_w_out,
           na_w_qkv, na_w_o, na_rpb, wa_w_qkv, wa_w_o, wa_sinks):
    bsz, seq, d = x.shape
    n_ctx = ctx.shape[1]
    depth = w_mod.shape[0]
    d_ff = ffn_w_out.shape[2]
    alpha = (2 * depth) ** 0.25
    t_lat, t_ctx = bsz * seq, bsz * n_ctx
    t_all = t_lat + t_ctx
    nt_lat, nt_all = t_lat // TOKEN_TILE, t_all // TOKEN_TILE
    tiles_per_seq = seq // TOKEN_TILE
    dq = N_HEADS * HEAD_DIM

    mod_rows = 16
    cc = jnp.concatenate([c, c_ctx[None], jnp.zeros((mod_rows - bsz - 1, d), F32)], axis=0)
    mods = _modulation(cc, w_mod, b_mod)

    w_in, w_out = ffn_w_in[0, 0].astype(BF16), ffn_w_out[0, 0].astype(BF16)
    ffn_w_specs = [_resident((d, 2 * d_ff)), _resident((d_ff, d))]
    h_parts = [x.reshape(t_lat, d), ctx.reshape(t_ctx, d)]
    ctx_blk0 = t_lat // n_ctx

    for i in range(depth):
        last = i == depth - 1
        is_na = i % 2 == 0
        j = i // 2
        mod_spec = _mod_spec(i, nt_lat, tiles_per_seq, bsz, d)
        ln_spec = pl.BlockSpec((None, 3, d), lambda t, i=i: (i, 0, 0))
        mixer_qkv, mixer_o = (na_w_qkv, na_w_o) if is_na else (wa_w_qkv, wa_w_o)

        h_specs = _split_tok_specs(d, nt_lat) if len(h_parts) == 2 else [_tok_spec(d)]
        jobs = [_cast_job(ffn_w_in, (i, 1), CAST_ROWS, nt_all), _cast_job(ffn_w_out, (i, 1), CAST_ROWS_OUT, nt_all),
                _cast_job(mixer_qkv, (j,), CAST_ROWS, nt_all), _cast_job(mixer_o, (j,), CAST_ROWS, nt_all)]
        c_in, c_out, c_shape, c_args = zip(*jobs)
        h, w_in, w_out, w_qkv, w_o = pl.pallas_call(
            functools.partial(_ffn_body, alpha=alpha, n_x=len(h_parts), n_lat_tiles=nt_lat, n_cast=len(jobs)),
            grid=(nt_all,),
            in_specs=[mod_spec, *h_specs, *ffn_w_specs, ln_spec, ln_spec, *c_in],
            out_specs=[_tok_spec(d), *c_out],
            out_shape=[jax.ShapeDtypeStruct((t_all, d), F32), *c_shape],
            compiler_params=_cparams(1),
            name=f"ffn_pre_{i}",
        )(mods, *h_parts, w_in, w_out, ln_g, ln_b, *c_args)

        if is_na:
            q, k, v = pl.pallas_call(
                _qkv_na_body,
                grid=(nt_all,),
                in_specs=[mod_spec, _tok_spec(d), _resident(w_qkv.shape)],
                out_specs=[_tok_spec(dq)] * 3,
                out_shape=[jax.ShapeDtypeStruct((t_all, dq), BF16)] * 3,
                compiler_params=_cparams(1),
                name=f"qkv_na_{i}",
            )(mods, h, w_qkv)
            att = _na_attention(q, k, v, _na_bias_table(na_rpb[j]), bsz, seq, n_ctx, f"attn_na_{i}")
        else:
            nkv = WA_KV_HEADS * LANES
            cos, sin = _rope_tables(seq, TOKEN_TILE)
            rope_spec = pl.BlockSpec(
                (TOKEN_TILE, LANES), lambda t: (jnp.where(t < nt_lat, t % tiles_per_seq, tiles_per_seq), 0))
            q, k, v = pl.pallas_call(
                _qkv_wa_body,
                grid=(nt_all,),
                in_specs=[mod_spec, _tok_spec(d), rope_spec, rope_spec, _resident(w_qkv.shape)],
                out_specs=[_tok_spec(dq), _tok_spec(nkv), _tok_spec(nkv)],
                out_shape=[jax.ShapeDtypeStruct((t_all, dq), BF16),
                           jax.ShapeDtypeStruct((t_all, nkv), BF16),
                           jax.ShapeDtypeStruct((t_all, nkv), BF16)],
                compiler_params=_cparams(1),
                name=f"qkv_wa_{i}",
            )(mods, h, cos, sin, w_qkv)
            att = _wa_attention(wa_sinks[j], q, k, v, bsz, seq, n_ctx, f"attn_wa_{i}")

        att_parts = [att]
        if not last:
            ctx_spec = pl.BlockSpec((n_ctx, dq), lambda b: (ctx_blk0 + b, 0))
            att_parts.append(pl.pallas_call(
                _ctx_body,
                grid=(bsz,),
                in_specs=[ctx_spec] * 3,
                out_specs=pl.BlockSpec((n_ctx, dq), lambda b: (b, 0)),
                out_shape=jax.ShapeDtypeStruct((t_ctx, dq), BF16),
                compiler_params=_cparams(1),
                name=f"attn_ctx_{i}",
            )(q, k, v))

        n_tiles = nt_lat if last else nt_all
        att_specs = _split_tok_specs(dq, nt_lat) if len(att_parts) == 2 else [_tok_spec(dq)]
        jobs = [] if last else [_cast_job(ffn_w_in, (i + 1, 0), CAST_ROWS, n_tiles),
                                _cast_job(ffn_w_out, (i + 1, 0), CAST_ROWS_OUT, n_tiles)]
        c_in, c_out, c_shape, c_args = zip(*jobs) if jobs else ((), (), (), ())
        h, *next_w = pl.pallas_call(
            functools.partial(_post_body, alpha=alpha, n_a=len(att_parts), n_lat_tiles=nt_lat, n_cast=len(jobs)),
            grid=(n_tiles,),
            in_specs=[mod_spec, _tok_spec(d), *att_specs, _resident(w_o.shape), *ffn_w_specs,
                      ln_spec, ln_spec, *c_in],
            out_specs=[_tok_spec(d), *c_out],
            out_shape=[jax.ShapeDtypeStruct((n_tiles * TOKEN_TILE, d), F32), *c_shape],
            compiler_params=_cparams(1),
            name=f"post_{i}",
        )(mods, h, *att_parts, w_o, w_in, w_out, ln_g, ln_b, *c_args)
        if next_w:
            w_in, w_out = next_w
        h_parts = [h]

    return h.reshape(bsz, seq, d)
```

```python
import functools

import numpy as np
import jax
import jax.numpy as jnp
from jax import lax
from jax.experimental import pallas as pl
from jax.experimental.pallas import tpu as pltpu

F32 = jnp.float32
BF16 = jnp.bfloat16

GRID_W = 64
N_HEADS = 16
HEAD_DIM = 64
NA_KH = 8
NA_KW = 16
WA_KV_HEADS = 4
WA_WINDOW = 128
FFN_RES = 0.5
ROPE_BASE = 10000.0
N_MOD = 9
LN_EPS = 1e-5
NEG_INF = -1e30

LANES = 128
TOKEN_TILE = 512
MOD_COLS_PER_STEP = 3
CAST_ROWS = 32
CAST_ROWS_OUT = 176
FIRST_ROW_CHUNK = 128
ROW_CHUNK = 256
NA_ROWS_PER_STEP = 8
WA_BLOCKS_PER_STEP = 4
NA_LOOKAHEAD = 2
WA_LOOKAHEAD = 4
LOG2E = 1.4426950408889634
Q_SCALE = HEAD_DIM ** -0.5 * LOG2E
VMEM_LIMIT = 56 * 1024 * 1024


def _cparams(n_grid):
    return pltpu.CompilerParams(
        dimension_semantics=("arbitrary",) * n_grid, vmem_limit_bytes=VMEM_LIMIT)


def _resident(shape):
    nd = len(shape)
    return pl.BlockSpec(tuple(shape), lambda *_: (0,) * nd, pipeline_mode=pl.Buffered(1))


def _tokens(refs, n_lat_tiles, rows):
    if len(refs) == 1:
        return refs[0][rows, :]
    return jnp.where(pl.program_id(0) < n_lat_tiles, refs[0][rows, :], refs[1][rows, :])


def _row_chunks(n):
    bounds = [0] + list(range(FIRST_ROW_CHUNK, n, ROW_CHUNK)) + [n]
    return [slice(a, b) for a, b in zip(bounds[:-1], bounds[1:])]


def _layer_norm(z, g, b):
    mu = jnp.mean(z, axis=-1, keepdims=True)
    var = jnp.mean(z * z, axis=-1, keepdims=True) - mu * mu
    return (z - mu) * lax.rsqrt(var + LN_EPS) * g + b


def _swiglu(u_bf16, win_ref, wout_ref):
    d_ff = wout_ref.shape[0]
    a = jnp.dot(u_bf16, win_ref[:, :d_ff], preferred_element_type=F32)
    v = jnp.dot(u_bf16, win_ref[:, d_ff:], preferred_element_type=F32)
    g = (a * jax.nn.sigmoid(a) * v).astype(BF16)
    return jnp.dot(g, wout_ref[...], preferred_element_type=F32)


def _ffn_residual(x, mod, base, win_ref, wout_ref, alpha):
    shift, scale, gate = mod[base:base + 1], mod[base + 1:base + 2], mod[base + 2:base + 3]
    u = (x * (1.0 + scale) + shift).astype(BF16)
    y = _swiglu(u, win_ref, wout_ref)
    return alpha * x + (FFN_RES * gate) * y


def _mod_body(c_ref, w_ref, b_ref, o_ref):
    c = c_ref[...]
    s = (c * jax.nn.sigmoid(c)).astype(BF16)
    o_ref[0] = jnp.dot(s, w_ref[0].astype(BF16), preferred_element_type=F32) + b_ref[0]


def _modulation(cc, w_mod, b_mod):
    depth, d, nd = w_mod.shape
    r = cc.shape[0]
    cols = MOD_COLS_PER_STEP * d
    out = pl.pallas_call(
        _mod_body,
        grid=(depth, nd // cols),
        in_specs=[
            pl.BlockSpec((r, d), lambda i, j: (0, 0)),
            pl.BlockSpec((1, d, cols), lambda i, j: (i, 0, j)),
            pl.BlockSpec((1, 1, cols), lambda i, j: (i, 0, j)),
        ],
        out_specs=pl.BlockSpec((1, r, cols), lambda i, j: (i, 0, j)),
        out_shape=jax.ShapeDtypeStruct((depth, r, nd), F32),
        compiler_params=_cparams(2),
        name="modulation",
    )(cc, w_mod, b_mod.reshape(depth, 1, nd))
    return out.reshape(depth, r, nd // d, d)


def _split_cast_refs(refs, n_cast):
    if not n_cast:
        return refs, ()
    main = refs[:-(2 * n_cast + 1)] + (refs[-(n_cast + 1)],)
    return main, tuple(zip(refs[-(2 * n_cast + 1):-(n_cast + 1)], refs[-n_cast:]))


def _cast_weights(cast_refs):
    for src, dst in cast_refs:
        dst[...] = src[...].astype(BF16)


def _ffn_body(mod_ref, *refs, alpha, n_x, n_lat_tiles, n_cast):
    refs, cast_refs = _split_cast_refs(refs, n_cast)
    x_refs, (win_ref, wout_ref, g_ref, b_ref, o_ref) = refs[:n_x], refs[n_x:]
    for rows in _row_chunks(o_ref.shape[0]):
        z = _ffn_residual(_tokens(x_refs, n_lat_tiles, rows), mod_ref[0], 0, win_ref, wout_ref, alpha)
        o_ref[rows, :] = _layer_norm(z, g_ref[0:1], b_ref[0:1])
    _cast_weights(cast_refs)


def _post_body(mod_ref, h_ref, *refs, alpha, n_a, n_lat_tiles, n_cast):
    refs, cast_refs = _split_cast_refs(refs, n_cast)
    a_refs, (wo_ref, win_ref, wout_ref, g_ref, b_ref, o_ref) = refs[:n_a], refs[n_a:]

    chunks = _row_chunks(o_ref.shape[0])
    ys = [jnp.dot(_tokens(a_refs, n_lat_tiles, rows), wo_ref[...], preferred_element_type=F32)
          for rows in chunks]
    for rows, y in zip(chunks, ys):
        h = _layer_norm(alpha * h_ref[rows, :] + mod_ref[0][5:6] * y, g_ref[1:2], b_ref[1:2])
        z = _ffn_residual(h, mod_ref[0], 6, win_ref, wout_ref, alpha)
        o_ref[rows, :] = _layer_norm(z, g_ref[2:3], b_ref[2:3])
    _cast_weights(cast_refs)


def _qkv_na_body(mod_ref, h_ref, w_ref, q_ref, k_ref, v_ref):
    mod = mod_ref[0]
    d = h_ref.shape[1]
    u = (h_ref[...] * (1.0 + mod[4:5]) + mod[3:4]).astype(BF16)
    r = jnp.dot(u, w_ref[...], preferred_element_type=F32)
    q_ref[...] = (r[:, :d] * Q_SCALE).astype(BF16)
    k_ref[...] = r[:, d:2 * d].astype(BF16)
    v_ref[...] = r[:, 2 * d:].astype(BF16)


def _rope_lanes(x, cos, sin_signed, first_half):
    partner = jnp.where(first_half, pltpu.roll(x, LANES - 16, 1), pltpu.roll(x, 16, 1))
    return x * cos + partner * sin_signed


def _qkv_wa_body(mod_ref, h_ref, cos_ref, sin_ref, w_ref, q_ref, k_ref, v_ref):
    mod = mod_ref[0]
    d = h_ref.shape[1]
    nk = k_ref.shape[1]
    n_src = nk // (2 * LANES)
    for rows in _row_chunks(h_ref.shape[0]):
        u = (h_ref[rows, :] * (1.0 + mod[4:5]) + mod[3:4]).astype(BF16)
        r = jnp.dot(u, w_ref[...], preferred_element_type=F32)
        cos, sin = cos_ref[rows, :], sin_ref[rows, :]
        first_half = lax.broadcasted_iota(jnp.int32, cos.shape, 1) % 32 < 16
        for j in range(d // LANES):
            sl = slice(j * LANES, (j + 1) * LANES)
            q_ref[rows, sl] = (_rope_lanes(r[:, sl], cos, sin, first_half) * Q_SCALE).astype(BF16)
        lo = lax.broadcasted_iota(jnp.int32, cos.shape, 1) < HEAD_DIM
        for dst_ref, base, rope in ((k_ref, d, True), (v_ref, d + n_src * LANES, False)):
            for j in range(n_src):
                x = r[:, base + j * LANES:base + (j + 1) * LANES]
                if rope:
                    x = _rope_lanes(x, cos, sin, first_half)
                swapped = pltpu.roll(x, HEAD_DIM, 1)
                dst_ref[rows, 2 * j * LANES:(2 * j + 1) * LANES] = jnp.where(lo, x, swapped).astype(BF16)
                dst_ref[rows, (2 * j + 1) * LANES:(2 * j + 2) * LANES] = jnp.where(lo, swapped, x).astype(BF16)


def _mod_spec(layer, n_lat_tiles, tiles_per_seq, ctx_row, d):
    return pl.BlockSpec(
        (None, 1, N_MOD, d),
        lambda t: (layer, jnp.where(t < n_lat_tiles, t // tiles_per_seq, ctx_row), 0, 0))


def _tok_spec(width):
    return pl.BlockSpec((TOKEN_TILE, width), lambda t: (t, 0))


def _cast_job(weight, lead, rows_per_step, n_steps):
    lead = tuple(lead)
    rows, cols = weight.shape[len(lead):]
    n = rows // rows_per_step
    assert rows == n * rows_per_step and n_steps >= n
    in_spec = pl.BlockSpec((None,) * len(lead) + (rows_per_step, cols),
                           lambda t: lead + (jnp.minimum(t, n - 1), 0))
    out_spec = pl.BlockSpec((rows_per_step, cols), lambda t: (jnp.minimum(t, n - 1), 0))
    return in_spec, out_spec, jax.ShapeDtypeStruct((rows, cols), BF16), weight


def _split_tok_specs(width, n_lat_tiles):
    return [pl.BlockSpec((TOKEN_TILE, width), lambda t: (jnp.minimum(t, n_lat_tiles - 1), 0)),
            pl.BlockSpec((TOKEN_TILE, width), lambda t: (jnp.maximum(t - n_lat_tiles, 0), 0))]


def _split_heads(x):
    lo = lax.broadcasted_iota(jnp.int32, x.shape, 1) < HEAD_DIM
    zero = jnp.zeros_like(x)
    return jnp.concatenate([jnp.where(lo, x, zero), jnp.where(lo, zero, x)], axis=0)


def _merge_heads(o):
    n = o.shape[0] // 2
    lo = lax.broadcasted_iota(jnp.int32, (n, LANES), 1) < HEAD_DIM
    return jnp.where(lo, o[:n], o[n:])


def _qk(q, k):
    return lax.dot_general(q, k, (((1,), (1,)), ((), ())), preferred_element_type=F32)


def _lane_fold(op, *xs):
    parts = [x[:, i:i + LANES] for x in xs for i in range(0, x.shape[1], LANES)]
    return functools.reduce(op, parts)


def _softmax_numerators(s_win, s_ctx, sink):
    x_win, x_ctx = s_win.astype(BF16), s_ctx.astype(BF16)
    m = jnp.max(_lane_fold(jnp.maximum, x_win, x_ctx).astype(F32), axis=-1, keepdims=True)
    if sink is not None:
        m = jnp.maximum(m, sink)
    m_bf16 = m.astype(BF16)
    return jnp.exp2(x_win - m_bf16), jnp.exp2(x_ctx - m_bf16), m_bf16.astype(F32)


def _with_ones(v):
    return jnp.concatenate([v, jnp.ones_like(v)], axis=1)


def _normalise(o_sum, extra):
    l = o_sum[:, LANES:LANES + 1]
    if extra is not None:
        l = l + extra
    return o_sum[:, :LANES] / l


def _pipelined(n, scores, finish, lookahead):
    pending = [scores(i) for i in range(min(lookahead, n))]
    for i in range(n):
        if i + lookahead < n:
            pending.append(scores(i + lookahead))
        finish(i, *pending.pop(0))


def _na_body(q_ref, k_ref, v_ref, kc_ref, vc_ref, t2_ref, o_ref):
    rows = k_ref.shape[0] // GRID_W
    n_win = NA_KH * GRID_W
    n_rows = q_ref.shape[0] // GRID_W
    r0 = pl.program_id(1) * n_rows
    key_rows = [jnp.clip(r0 + i - NA_KH // 2, 0, rows - NA_KH) for i in range(n_rows)]
    k_offs = [pl.multiple_of(rs * GRID_W, GRID_W) for rs in key_rows]
    shifts = [rs - (r0 + i) + NA_KH - 1 for i, rs in enumerate(key_rows)]
    m_row = 2 * GRID_W

    def scores(p):
        sl = slice(p * LANES, (p + 1) * LANES)
        qs = [_split_heads(q_ref[i * GRID_W:(i + 1) * GRID_W, sl]) for i in range(n_rows)]
        s_win = jnp.concatenate(
            [_qk(qs[i], k_ref[pl.ds(k_offs[i], n_win), sl]) for i in range(n_rows)], axis=0)
        return s_win, _qk(jnp.concatenate(qs, axis=0), kc_ref[:, sl])

    def finish(p, s_win, s_ctx):
        sl = slice(p * LANES, (p + 1) * LANES)
        bias = jnp.concatenate(
            [jnp.concatenate(
                [jnp.concatenate([t2_ref[2 * p, sh + 2 * jj], t2_ref[2 * p + 1, sh + 2 * jj]], axis=0)
                 for jj in range(NA_KH // 2)], axis=1) for sh in shifts], axis=0)
        p_win, p_ctx, _ = _softmax_numerators(s_win + bias, s_ctx, None)
        o_win = jnp.concatenate(
            [jnp.dot(p_win[i * m_row:(i + 1) * m_row], _with_ones(v_ref[pl.ds(k_offs[i], n_win), sl]),
                     preferred_element_type=F32) for i in range(n_rows)], axis=0)
        o = _normalise(o_win + jnp.dot(p_ctx, _with_ones(vc_ref[:, sl]), preferred_element_type=F32), None)
        for i in range(n_rows):
            o_ref[i * GRID_W:(i + 1) * GRID_W, sl] = _merge_heads(o[i * m_row:(i + 1) * m_row]).astype(BF16)

    _pipelined(N_HEADS // 2, scores, finish, NA_LOOKAHEAD)


def _ctx_body(q_ref, k_ref, v_ref, o_ref):
    for p in range(N_HEADS // 2):
        sl = slice(p * LANES, (p + 1) * LANES)
        qs = _split_heads(q_ref[:, sl])
        s = _qk(qs, k_ref[:, sl])
        m = jnp.max(s, axis=-1, keepdims=True)
        e = jnp.exp2(s - m)
        l = jnp.sum(e, axis=-1, keepdims=True)
        o = jnp.dot(e.astype(BF16), v_ref[:, sl], preferred_element_type=F32)
        o_ref[:, sl] = _merge_heads(o / l).astype(BF16)


def _wa_body(sink_ref, q_ref, k_ref, v_ref, kc_ref, vc_ref, o_ref):
    seq = k_ref.shape[0]
    blk = WA_WINDOW
    n_blk = q_ref.shape[0] // blk
    n_win = 3 * blk
    qb0 = pl.program_id(1) * n_blk
    starts = [pl.multiple_of(jnp.clip((qb0 + b - 1) * blk, 0, seq - n_win), blk) for b in range(n_blk)]
    d_row = lax.broadcasted_iota(jnp.int32, (blk, n_win), 0) - lax.broadcasted_iota(jnp.int32, (blk, n_win), 1)
    bands = []
    for b in range(n_blk):
        dist = d_row + ((qb0 + b) * blk - starts[b])
        band = jnp.where(jnp.abs(dist) <= WA_WINDOW, 0.0, NEG_INF).astype(F32)
        bands += [band, band]
    band = jnp.concatenate(bands, axis=0)
    pairs_per_kv = N_HEADS // WA_KV_HEADS // 2
    m_blk = 2 * blk

    def scores(p):
        kv = slice(p // pairs_per_kv * LANES, (p // pairs_per_kv + 1) * LANES)
        qs = [_split_heads(q_ref[b * blk:(b + 1) * blk, p * LANES:(p + 1) * LANES]) for b in range(n_blk)]
        s_win = jnp.concatenate(
            [_qk(qs[b], k_ref[pl.ds(starts[b], n_win), kv]) for b in range(n_blk)], axis=0)
        return s_win, _qk(jnp.concatenate(qs, axis=0), kc_ref[:, kv])

    def finish(p, s_win, s_ctx):
        kv = slice(p // pairs_per_kv * LANES, (p // pairs_per_kv + 1) * LANES)
        sink = jnp.concatenate(
            [jnp.full((blk, 1), sink_ref[2 * p + j] * LOG2E, F32) for j in range(2)] * n_blk, axis=0)
        p_win, p_ctx, m = _softmax_numerators(s_win + band, s_ctx, sink)
        o_win = jnp.concatenate(
            [jnp.dot(p_win[b * m_blk:(b + 1) * m_blk], _with_ones(v_ref[pl.ds(starts[b], n_win), kv]),
                     preferred_element_type=F32) for b in range(n_blk)], axis=0)
        o = _normalise(o_win + jnp.dot(p_ctx, _with_ones(vc_ref[:, kv]), preferred_element_type=F32),
                       jnp.exp2(sink - m))
        for b in range(n_blk):
            o_ref[b * blk:(b + 1) * blk, p * LANES:(p + 1) * LANES] = _merge_heads(
                o[b * m_blk:(b + 1) * m_blk]).astype(BF16)

    _pipelined(N_HEADS // 2, scores, finish, WA_LOOKAHEAD)


def _attention_specs(q_rows, dq, nkv, bsz, seq, n_ctx):
    steps = seq // q_rows
    ctx_blk0 = bsz * seq // n_ctx
    q_spec = pl.BlockSpec((q_rows, dq), lambda b, r: (b * steps + r, 0))
    kv_spec = pl.BlockSpec((seq, nkv), lambda b, r: (b, 0))
    ctx_spec = pl.BlockSpec((n_ctx, nkv), lambda b, r: (ctx_blk0 + b, 0))
    return (bsz, steps), q_spec, [kv_spec, kv_spec, ctx_spec, ctx_spec]


def _na_attention(q, k, v, t2, bsz, seq, n_ctx, name):
    dq = q.shape[1]
    grid, q_spec, kv_specs = _attention_specs(NA_ROWS_PER_STEP * GRID_W, dq, dq, bsz, seq, n_ctx)
    return pl.pallas_call(
        _na_body,
        grid=grid,
        in_specs=[q_spec, *kv_specs, _resident(t2.shape)],
        out_specs=q_spec,
        out_shape=jax.ShapeDtypeStruct((bsz * seq, dq), BF16),
        compiler_params=_cparams(2),
        name=name,
    )(q, k, v, k, v, t2)


def _wa_attention(sinks, q, k, v, bsz, seq, n_ctx, name):
    dq = q.shape[1]
    grid, q_spec, kv_specs = _attention_specs(WA_BLOCKS_PER_STEP * WA_WINDOW, dq, k.shape[1], bsz, seq, n_ctx)
    return pl.pallas_call(
        _wa_body,
        grid=grid,
        in_specs=[pl.BlockSpec(memory_space=pltpu.SMEM), q_spec, *kv_specs],
        out_specs=q_spec,
        out_shape=jax.ShapeDtypeStruct((bsz * seq, dq), BF16),
        compiler_params=_cparams(2),
        name=name,
    )(sinks, q, k, v, k, v)


def _na_bias_table(rpb):
    qcol = np.arange(GRID_W)[:, None]
    kcol = np.arange(GRID_W)[None, :]
    win_start = np.clip(qcol - NA_KW // 2, 0, GRID_W - NA_KW)
    valid = (kcol >= win_start) & (kcol < win_start + NA_KW)
    off = np.clip(kcol - qcol, -(NA_KW - 1), NA_KW - 1) + NA_KW - 1
    n_off = 2 * NA_KW - 1
    onehot = (off[None] == np.arange(n_off)[:, None, None]).astype(np.float32)
    pair = np.zeros((2 * n_off, GRID_W, 2 * GRID_W), np.float32)
    pair[:n_off, :, :GRID_W] = onehot
    pair[n_off:, :, GRID_W:] = onehot
    rpb2 = jnp.concatenate([rpb[:, :-1], rpb[:, 1:]], axis=-1).astype(F32)
    t = jnp.einsum("hdo,oqk->hdqk", rpb2, pair, precision=lax.Precision.HIGHEST)
    return jnp.where(np.concatenate([valid, valid], axis=1), t * LOG2E, NEG_INF)


def _rope_tables(seq, pad_rows):
    t = np.arange(seq)
    n = HEAD_DIM // 4
    inv_freq = ROPE_BASE ** (-np.arange(n, dtype=np.float64) / n)
    ang_r = (t // GRID_W)[:, None] * inv_freq[None, :]
    ang_c = (t % GRID_W)[:, None] * inv_freq[None, :]
    ang = np.concatenate([ang_r, ang_r, ang_c, ang_c], axis=1)
    sign = np.tile(np.concatenate([-np.ones(n), np.ones(n)]), 2)
    cos = np.concatenate([np.cos(ang), np.ones((pad_rows, HEAD_DIM))], axis=0)
    sin = np.concatenate([np.sin(ang) * sign, np.zeros((pad_rows, HEAD_DIM))], axis=0)
    reps = (1, LANES // HEAD_DIM)
    return jnp.asarray(np.tile(cos, reps), F32), jnp.asarray(np.tile(sin, reps), F32)


def kernel(x, c, ctx, c_ctx, w_mod, b_mod, ln_g, ln_b, ffn_w_in, ffn_w_out,
           na_w_qkv, na_w_o, na_rpb, wa_w_qkv, wa_w_o, wa_sinks):
    bsz, seq, d = x.shape
    n_ctx = ctx.shape[1]
    depth = w_mod.shape[0]
    d_ff = ffn_w_out.shape[2]
    alpha = (2 * depth) ** 0.25
    t_lat, t_ctx = bsz * seq, bsz * n_ctx
    t_all = t_lat + t_ctx
    nt_lat, nt_all = t_lat // TOKEN_TILE, t_all // TOKEN_TILE
    tiles_per_seq = seq // TOKEN_TILE
    dq = N_HEADS * HEAD_DIM

    mod_rows = 16
    cc = jnp.concatenate([c, c_ctx[None], jnp.zeros((mod_rows - bsz - 1, d), F32)], axis=0)
    mods = _modulation(cc, w_mod, b_mod)

    w_in, w_out = ffn_w_in[0, 0].astype(BF16), ffn_w_out[0, 0].astype(BF16)
    ffn_w_specs = [_resident((d, 2 * d_ff)), _resident((d_ff, d))]
    h_parts = [x.reshape(t_lat, d), ctx.reshape(t_ctx, d)]
    ctx_blk0 = t_lat // n_ctx

    for i in range(depth):
        last = i == depth - 1
        is_na = i % 2 == 0
        j = i // 2
        mod_spec = _mod_spec(i, nt_lat, tiles_per_seq, bsz, d)
        ln_spec = pl.BlockSpec((None, 3, d), lambda t, i=i: (i, 0, 0))
        mixer_qkv, mixer_o = (na_w_qkv, na_w_o) if is_na else (wa_w_qkv, wa_w_o)

        h_specs = _split_tok_specs(d, nt_lat) if len(h_parts) == 2 else [_tok_spec(d)]
        jobs = [_cast_job(ffn_w_in, (i, 1), CAST_ROWS, nt_all), _cast_job(ffn_w_out, (i, 1), CAST_ROWS_OUT, nt_all),
                _cast_job(mixer_qkv, (j,), CAST_ROWS, nt_all), _cast_job(mixer_o, (j,), CAST_ROWS, nt_all)]
        c_in, c_out, c_shape, c_args = zip(*jobs)
        h, w_in, w_out, w_qkv, w_o = pl.pallas_call(
            functools.partial(_ffn_body, alpha=alpha, n_x=len(h_parts), n_lat_tiles=nt_lat, n_cast=len(jobs)),
            grid=(nt_all,),
            in_specs=[mod_spec, *h_specs, *ffn_w_specs, ln_spec, ln_spec, *c_in],
            out_specs=[_tok_spec(d), *c_out],
            out_shape=[jax.ShapeDtypeStruct((t_all, d), F32), *c_shape],
            compiler_params=_cparams(1),
            name=f"ffn_pre_{i}",
        )(mods, *h_parts, w_in, w_out, ln_g, ln_b, *c_args)

        if is_na:
            q, k, v = pl.pallas_call(
                _qkv_na_body,
                grid=(nt_all,),
                in_specs=[mod_spec, _tok_spec(d), _resident(w_qkv.shape)],
                out_specs=[_tok_spec(dq)] * 3,
                out_shape=[jax.ShapeDtypeStruct((t_all, dq), BF16)] * 3,
                compiler_params=_cparams(1),
                name=f"qkv_na_{i}",
            )(mods, h, w_qkv)
            att = _na_attention(q, k, v, _na_bias_table(na_rpb[j]), bsz, seq, n_ctx, f"attn_na_{i}")
        else:
            nkv = WA_KV_HEADS * LANES
            cos, sin = _rope_tables(seq, TOKEN_TILE)
            rope_spec = pl.BlockSpec(
                (TOKEN_TILE, LANES), lambda t: (jnp.where(t < nt_lat, t % tiles_per_seq, tiles_per_seq), 0))
            q, k, v = pl.pallas_call(
                _qkv_wa_body,
                grid=(nt_all,),
                in_specs=[mod_spec, _tok_spec(d), rope_spec, rope_spec, _resident(w_qkv.shape)],
                out_specs=[_tok_spec(dq), _tok_spec(nkv), _tok_spec(nkv)],
                out_shape=[jax.ShapeDtypeStruct((t_all, dq), BF16),
                           jax.ShapeDtypeStruct((t_all, nkv), BF16),
                           jax.ShapeDtypeStruct((t_all, nkv), BF16)],
                compiler_params=_cparams(1),
                name=f"qkv_wa_{i}",
            )(mods, h, cos, sin, w_qkv)
            att = _wa_attention(wa_sinks[j], q, k, v, bsz, seq, n_ctx, f"attn_wa_{i}")

        att_parts = [att]
        if not last:
            ctx_spec = pl.BlockSpec((n_ctx, dq), lambda b: (ctx_blk0 + b, 0))
            att_parts.append(pl.pallas_call(
                _ctx_body,
                grid=(bsz,),
                in_specs=[ctx_spec] * 3,
                out_specs=pl.BlockSpec((n_ctx, dq), lambda b: (b, 0)),
                out_shape=jax.ShapeDtypeStruct((t_ctx, dq), BF16),
                compiler_params=_cparams(1),
                name=f"attn_ctx_{i}",
            )(q, k, v))

        n_tiles = nt_lat if last else nt_all
        att_specs = _split_tok_specs(dq, nt_lat) if len(att_parts) == 2 else [_tok_spec(dq)]
        jobs = [] if last else [_cast_job(ffn_w_in, (i + 1, 0), CAST_ROWS, n_tiles),
                                _cast_job(ffn_w_out, (i + 1, 0), CAST_ROWS_OUT, n_tiles)]
        c_in, c_out, c_shape, c_args = zip(*jobs) if jobs else ((), (), (), ())
        h, *next_w = pl.pallas_call(
            functools.partial(_post_body, alpha=alpha, n_a=len(att_parts), n_lat_tiles=nt_lat, n_cast=len(jobs)),
            grid=(n_tiles,),
            in_specs=[mod_spec, _tok_spec(d), *att_specs, _resident(w_o.shape), *ffn_w_specs,
                      ln_spec, ln_spec, *c_in],
            out_specs=[_tok_spec(d), *c_out],
            out_shape=[jax.ShapeDtypeStruct((n_tiles * TOKEN_TILE, d), F32), *c_shape],
            compiler_params=_cparams(1),
            name=f"post_{i}",
        )(mods, h, *att_parts, w_o, w_in, w_out, ln_g, ln_b, *c_args)
        if next_w:
            w_in, w_out = next_w
        h_parts = [h]

    return h.reshape(bsz, seq, d)
```

```python
import functools

import numpy as np
import jax
import jax.numpy as jnp
from jax import lax
from jax.experimental import pallas as pl
from jax.experimental.pallas import tpu as pltpu

F32 = jnp.float32
BF16 = jnp.bfloat16

GRID_W = 64
N_HEADS = 16
HEAD_DIM = 64
NA_KH = 8
NA_KW = 16
WA_KV_HEADS = 4
WA_WINDOW = 128
FFN_RES = 0.5
ROPE_BASE = 10000.0
N_MOD = 9
LN_EPS = 1e-5
NEG_INF = -1e30

LANES = 128
TOKEN_TILE = 512
MOD_COLS_PER_STEP = 3
CAST_ROWS = 32
CAST_ROWS_OUT = 176
FIRST_ROW_CHUNK = 128
ROW_CHUNK = 256
FF_GROUP_COLS = 768
NA_ROWS_PER_STEP = 8
WA_BLOCKS_PER_STEP = 4
NA_LOOKAHEAD = 2
WA_LOOKAHEAD = 4
LOG2E = 1.4426950408889634
Q_SCALE = HEAD_DIM ** -0.5 * LOG2E
VMEM_LIMIT = 56 * 1024 * 1024


def _cparams(n_grid):
    return pltpu.CompilerParams(
        dimension_semantics=("arbitrary",) * n_grid, vmem_limit_bytes=VMEM_LIMIT)


def _resident(shape):
    nd = len(shape)
    return pl.BlockSpec(tuple(shape), lambda *_: (0,) * nd, pipeline_mode=pl.Buffered(1))


def _tokens(refs, n_lat_tiles, rows):
    if len(refs) == 1:
        return refs[0][rows, :]
    return jnp.where(pl.program_id(0) < n_lat_tiles, refs[0][rows, :], refs[1][rows, :])


def _row_chunks(n):
    bounds = [0] + list(range(FIRST_ROW_CHUNK, n, ROW_CHUNK)) + [n]
    return [slice(a, b) for a, b in zip(bounds[:-1], bounds[1:])]


def _layer_norm(z, g, b):
    mu = jnp.mean(z, axis=-1, keepdims=True)
    var = jnp.mean(z * z, axis=-1, keepdims=True) - mu * mu
    return (z - mu) * lax.rsqrt(var + LN_EPS) * g + b


def _swiglu(u_bf16, win_ref, wout_ref):
    d_ff = wout_ref.shape[0]
    bounds = list(range(0, d_ff, FF_GROUP_COLS)) + [d_ff]
    gs = []
    for c0, c1 in zip(bounds[:-1], bounds[1:]):
        a = jnp.dot(u_bf16, win_ref[:, c0:c1], preferred_element_type=F32)
        v = jnp.dot(u_bf16, win_ref[:, d_ff + c0:d_ff + c1], preferred_element_type=F32)
        gs.append((a * jax.nn.sigmoid(a) * v).astype(BF16))
    return jnp.dot(jnp.concatenate(gs, axis=1), wout_ref[...], preferred_element_type=F32)


def _ffn_residual(x, mod, base, win_ref, wout_ref, alpha):
    shift, scale, gate = mod[base:base + 1], mod[base + 1:base + 2], mod[base + 2:base + 3]
    u = (x * (1.0 + scale) + shift).astype(BF16)
    y = _swiglu(u, win_ref, wout_ref)
    return alpha * x + (FFN_RES * gate) * y


def _mod_body(c_ref, w_ref, b_ref, o_ref):
    c = c_ref[...]
    s = (c * jax.nn.sigmoid(c)).astype(BF16)
    o_ref[0] = jnp.dot(s, w_ref[0].astype(BF16), preferred_element_type=F32) + b_ref[0]


def _modulation(cc, w_mod, b_mod):
    depth, d, nd = w_mod.shape
    r = cc.shape[0]
    cols = MOD_COLS_PER_STEP * d
    out = pl.pallas_call(
        _mod_body,
        grid=(depth, nd // cols),
        in_specs=[
            pl.BlockSpec((r, d), lambda i, j: (0, 0)),
            pl.BlockSpec((1, d, cols), lambda i, j: (i, 0, j)),
            pl.BlockSpec((1, 1, cols), lambda i, j: (i, 0, j)),
        ],
        out_specs=pl.BlockSpec((1, r, cols), lambda i, j: (i, 0, j)),
        out_shape=jax.ShapeDtypeStruct((depth, r, nd), F32),
        compiler_params=_cparams(2),
        name="modulation",
    )(cc, w_mod, b_mod.reshape(depth, 1, nd))
    return out.reshape(depth, r, nd // d, d)


def _split_cast_refs(refs, n_cast):
    if not n_cast:
        return refs, ()
    main = refs[:-(2 * n_cast + 1)] + (refs[-(n_cast + 1)],)
    return main, tuple(zip(refs[-(2 * n_cast + 1):-(n_cast + 1)], refs[-n_cast:]))


def _cast_weights(cast_refs):
    for src, dst in cast_refs:
        dst[...] = src[...].astype(BF16)


def _ffn_body(mod_ref, *refs, alpha, n_x, n_lat_tiles, n_cast):
    refs, cast_refs = _split_cast_refs(refs, n_cast)
    x_refs, (win_ref, wout_ref, g_ref, b_ref, o_ref) = refs[:n_x], refs[n_x:]
    for rows in _row_chunks(o_ref.shape[0]):
        z = _ffn_residual(_tokens(x_refs, n_lat_tiles, rows), mod_ref[0], 0, win_ref, wout_ref, alpha)
        o_ref[rows, :] = _layer_norm(z, g_ref[0:1], b_ref[0:1])
    _cast_weights(cast_refs)


def _post_body(mod_ref, h_ref, *refs, alpha, n_a, n_lat_tiles, n_cast):
    refs, cast_refs = _split_cast_refs(refs, n_cast)
    a_refs, (wo_ref, win_ref, wout_ref, g_ref, b_ref, o_ref) = refs[:n_a], refs[n_a:]

    chunks = _row_chunks(o_ref.shape[0])
    ys = [jnp.dot(_tokens(a_refs, n_lat_tiles, rows), wo_ref[...], preferred_element_type=F32)
          for rows in chunks]
    for rows, y in zip(chunks, ys):
        h = _layer_norm(alpha * h_ref[rows, :] + mod_ref[0][5:6] * y, g_ref[1:2], b_ref[1:2])
        z = _ffn_residual(h, mod_ref[0], 6, win_ref, wout_ref, alpha)
        o_ref[rows, :] = _layer_norm(z, g_ref[2:3], b_ref[2:3])
    _cast_weights(cast_refs)


def _qkv_na_body(mod_ref, h_ref, w_ref, q_ref, k_ref, v_ref):
    mod = mod_ref[0]
    d = h_ref.shape[1]
    u = (h_ref[...] * (1.0 + mod[4:5]) + mod[3:4]).astype(BF16)
    r = jnp.dot(u, w_ref[...], preferred_element_type=F32)
    q_ref[...] = (r[:, :d] * Q_SCALE).astype(BF16)
    k_ref[...] = r[:, d:2 * d].astype(BF16)
    v_ref[...] = r[:, 2 * d:].astype(BF16)


def _rope_lanes(x, cos, sin_signed, first_half):
    partner = jnp.where(first_half, pltpu.roll(x, LANES - 16, 1), pltpu.roll(x, 16, 1))
    return x * cos + partner * sin_signed


def _qkv_wa_body(mod_ref, h_ref, cos_ref, sin_ref, w_ref, q_ref, k_ref, v_ref):
    mod = mod_ref[0]
    d = h_ref.shape[1]
    nk = k_ref.shape[1]
    n_src = nk // (2 * LANES)
    for rows in _row_chunks(h_ref.shape[0]):
        u = (h_ref[rows, :] * (1.0 + mod[4:5]) + mod[3:4]).astype(BF16)
        r = jnp.dot(u, w_ref[...], preferred_element_type=F32)
        cos, sin = cos_ref[rows, :], sin_ref[rows, :]
        first_half = lax.broadcasted_iota(jnp.int32, cos.shape, 1) % 32 < 16
        for j in range(d // LANES):
            sl = slice(j * LANES, (j + 1) * LANES)
            q_ref[rows, sl] = (_rope_lanes(r[:, sl], cos, sin, first_half) * Q_SCALE).astype(BF16)
        lo = lax.broadcasted_iota(jnp.int32, cos.shape, 1) < HEAD_DIM
        for dst_ref, base, rope in ((k_ref, d, True), (v_ref, d + n_src * LANES, False)):
            for j in range(n_src):
                x = r[:, base + j * LANES:base + (j + 1) * LANES]
                if rope:
                    x = _rope_lanes(x, cos, sin, first_half)
                swapped = pltpu.roll(x, HEAD_DIM, 1)
                dst_ref[rows, 2 * j * LANES:(2 * j + 1) * LANES] = jnp.where(lo, x, swapped).astype(BF16)
                dst_ref[rows, (2 * j + 1) * LANES:(2 * j + 2) * LANES] = jnp.where(lo, swapped, x).astype(BF16)


def _mod_spec(layer, n_lat_tiles, tiles_per_seq, ctx_row, d):
    return pl.BlockSpec(
        (None, 1, N_MOD, d),
        lambda t: (layer, jnp.where(t < n_lat_tiles, t // tiles_per_seq, ctx_row), 0, 0))


def _tok_spec(width):
    return pl.BlockSpec((TOKEN_TILE, width), lambda t: (t, 0))


def _cast_job(weight, lead, rows_per_step, n_steps):
    lead = tuple(lead)
    rows, cols = weight.shape[len(lead):]
    n = rows // rows_per_step
    assert rows == n * rows_per_step and n_steps >= n
    in_spec = pl.BlockSpec((None,) * len(lead) + (rows_per_step, cols),
                           lambda t: lead + (jnp.minimum(t, n - 1), 0))
    out_spec = pl.BlockSpec((rows_per_step, cols), lambda t: (jnp.minimum(t, n - 1), 0))
    return in_spec, out_spec, jax.ShapeDtypeStruct((rows, cols), BF16), weight


def _split_tok_specs(width, n_lat_tiles):
    return [pl.BlockSpec((TOKEN_TILE, width), lambda t: (jnp.minimum(t, n_lat_tiles - 1), 0)),
            pl.BlockSpec((TOKEN_TILE, width), lambda t: (jnp.maximum(t - n_lat_tiles, 0), 0))]


def _split_heads(x):
    lo = lax.broadcasted_iota(jnp.int32, x.shape, 1) < HEAD_DIM
    zero = jnp.zeros_like(x)
    return jnp.concatenate([jnp.where(lo, x, zero), jnp.where(lo, zero, x)], axis=0)


def _merge_heads(o):
    n = o.shape[0] // 2
    lo = lax.broadcasted_iota(jnp.int32, (n, LANES), 1) < HEAD_DIM
    return jnp.where(lo, o[:n], o[n:])


def _qk(q, k):
    return lax.dot_general(q, k, (((1,), (1,)), ((), ())), preferred_element_type=F32)


def _lane_fold(op, *xs):
    parts = [x[:, i:i + LANES] for x in xs for i in range(0, x.shape[1], LANES)]
    return functools.reduce(op, parts)


def _softmax_numerators(s_win, s_ctx, sink):
    x_win, x_ctx = s_win.astype(BF16), s_ctx.astype(BF16)
    m = jnp.max(_lane_fold(jnp.maximum, x_win, x_ctx).astype(F32), axis=-1, keepdims=True)
    if sink is not None:
        m = jnp.maximum(m, sink)
    m_bf16 = m.astype(BF16)
    return jnp.exp2(x_win - m_bf16), jnp.exp2(x_ctx - m_bf16), m_bf16.astype(F32)


def _with_ones(v):
    return jnp.concatenate([v, jnp.ones_like(v)], axis=1)


def _normalise(o_sum, extra):
    l = o_sum[:, LANES:LANES + 1]
    if extra is not None:
        l = l + extra
    return o_sum[:, :LANES] / l


def _pipelined(n, scores, finish, lookahead):
    pending = [scores(i) for i in range(min(lookahead, n))]
    for i in range(n):
        if i + lookahead < n:
            pending.append(scores(i + lookahead))
        finish(i, *pending.pop(0))


def _na_body(q_ref, k_ref, v_ref, kc_ref, vc_ref, t2_ref, o_ref):
    rows = k_ref.shape[0] // GRID_W
    n_win = NA_KH * GRID_W
    n_rows = q_ref.shape[0] // GRID_W
    r0 = pl.program_id(1) * n_rows
    key_rows = [jnp.clip(r0 + i - NA_KH // 2, 0, rows - NA_KH) for i in range(n_rows)]
    k_offs = [pl.multiple_of(rs * GRID_W, GRID_W) for rs in key_rows]
    shifts = [rs - (r0 + i) + NA_KH - 1 for i, rs in enumerate(key_rows)]
    m_row = 2 * GRID_W

    def scores(p):
        sl = slice(p * LANES, (p + 1) * LANES)
        qs = [_split_heads(q_ref[i * GRID_W:(i + 1) * GRID_W, sl]) for i in range(n_rows)]
        s_win = jnp.concatenate(
            [_qk(qs[i], k_ref[pl.ds(k_offs[i], n_win), sl]) for i in range(n_rows)], axis=0)
        return s_win, _qk(jnp.concatenate(qs, axis=0), kc_ref[:, sl])

    def finish(p, s_win, s_ctx):
        sl = slice(p * LANES, (p + 1) * LANES)
        bias = jnp.concatenate(
            [jnp.concatenate(
                [jnp.concatenate([t2_ref[2 * p, sh + 2 * jj], t2_ref[2 * p + 1, sh + 2 * jj]], axis=0)
                 for jj in range(NA_KH // 2)], axis=1) for sh in shifts], axis=0)
        p_win, p_ctx, _ = _softmax_numerators(s_win + bias, s_ctx, None)
        o_win = jnp.concatenate(
            [jnp.dot(p_win[i * m_row:(i + 1) * m_row], _with_ones(v_ref[pl.ds(k_offs[i], n_win), sl]),
                     preferred_element_type=F32) for i in range(n_rows)], axis=0)
        o = _normalise(o_win + jnp.dot(p_ctx, _with_ones(vc_ref[:, sl]), preferred_element_type=F32), None)
        for i in range(n_rows):
            o_ref[i * GRID_W:(i + 1) * GRID_W, sl] = _merge_heads(o[i * m_row:(i + 1) * m_row]).astype(BF16)

    _pipelined(N_HEADS // 2, scores, finish, NA_LOOKAHEAD)


def _ctx_body(q_ref, k_ref, v_ref, o_ref):
    for p in range(N_HEADS // 2):
        sl = slice(p * LANES, (p + 1) * LANES)
        qs = _split_heads(q_ref[:, sl])
        s = _qk(qs, k_ref[:, sl])
        m = jnp.max(s, axis=-1, keepdims=True)
        e = jnp.exp2(s - m)
        l = jnp.sum(e, axis=-1, keepdims=True)
        o = jnp.dot(e.astype(BF16), v_ref[:, sl], preferred_element_type=F32)
        o_ref[:, sl] = _merge_heads(o / l).astype(BF16)


def _wa_body(sink_ref, q_ref, k_ref, v_ref, kc_ref, vc_ref, o_ref):
    seq = k_ref.shape[0]
    blk = WA_WINDOW
    n_blk = q_ref.shape[0] // blk
    n_win = 3 * blk
    qb0 = pl.program_id(1) * n_blk
    starts = [pl.multiple_of(jnp.clip((qb0 + b - 1) * blk, 0, seq - n_win), blk) for b in range(n_blk)]
    d_row = lax.broadcasted_iota(jnp.int32, (blk, n_win), 0) - lax.broadcasted_iota(jnp.int32, (blk, n_win), 1)
    bands = []
    for b in range(n_blk):
        dist = d_row + ((qb0 + b) * blk - starts[b])
        band = jnp.where(jnp.abs(dist) <= WA_WINDOW, 0.0, NEG_INF).astype(F32)
        bands += [band, band]
    band = jnp.concatenate(bands, axis=0)
    pairs_per_kv = N_HEADS // WA_KV_HEADS // 2
    m_blk = 2 * blk

    def scores(p):
        kv = slice(p // pairs_per_kv * LANES, (p // pairs_per_kv + 1) * LANES)
        qs = [_split_heads(q_ref[b * blk:(b + 1) * blk, p * LANES:(p + 1) * LANES]) for b in range(n_blk)]
        s_win = jnp.concatenate(
            [_qk(qs[b], k_ref[pl.ds(starts[b], n_win), kv]) for b in range(n_blk)], axis=0)
        return s_win, _qk(jnp.concatenate(qs, axis=0), kc_ref[:, kv])

    def finish(p, s_win, s_ctx):
        kv = slice(p // pairs_per_kv * LANES, (p // pairs_per_kv + 1) * LANES)
        sink = jnp.concatenate(
            [jnp.full((blk, 1), sink_ref[2 * p + j] * LOG2E, F32) for j in range(2)] * n_blk, axis=0)
        p_win, p_ctx, m = _softmax_numerators(s_win + band, s_ctx, sink)
        o_win = jnp.concatenate(
            [jnp.dot(p_win[b * m_blk:(b + 1) * m_blk], _with_ones(v_ref[pl.ds(starts[b], n_win), kv]),
                     preferred_element_type=F32) for b in range(n_blk)], axis=0)
        o = _normalise(o_win + jnp.dot(p_ctx, _with_ones(vc_ref[:, kv]), preferred_element_type=F32),
                       jnp.exp2(sink - m))
        for b in range(n_blk):
            o_ref[b * blk:(b + 1) * blk, p * LANES:(p + 1) * LANES] = _merge_heads(
                o[b * m_blk:(b + 1) * m_blk]).astype(BF16)

    _pipelined(N_HEADS // 2, scores, finish, WA_LOOKAHEAD)


def _attention_specs(q_rows, dq, nkv, bsz, seq, n_ctx):
    steps = seq // q_rows
    ctx_blk0 = bsz * seq // n_ctx
    q_spec = pl.BlockSpec((q_rows, dq), lambda b, r: (b * steps + r, 0))
    kv_spec = pl.BlockSpec((seq, nkv), lambda b, r: (b, 0))
    ctx_spec = pl.BlockSpec((n_ctx, nkv), lambda b, r: (ctx_blk0 + b, 0))
    return (bsz, steps), q_spec, [kv_spec, kv_spec, ctx_spec, ctx_spec]


def _na_attention(q, k, v, t2, bsz, seq, n_ctx, name):
    dq = q.shape[1]
    grid, q_spec, kv_specs = _attention_specs(NA_ROWS_PER_STEP * GRID_W, dq, dq, bsz, seq, n_ctx)
    return pl.pallas_call(
        _na_body,
        grid=grid,
        in_specs=[q_spec, *kv_specs, _resident(t2.shape)],
        out_specs=q_spec,
        out_shape=jax.ShapeDtypeStruct((bsz * seq, dq), BF16),
        compiler_params=_cparams(2),
        name=name,
    )(q, k, v, k, v, t2)


def _wa_attention(sinks, q, k, v, bsz, seq, n_ctx, name):
    dq = q.shape[1]
    grid, q_spec, kv_specs = _attention_specs(WA_BLOCKS_PER_STEP * WA_WINDOW, dq, k.shape[1], bsz, seq, n_ctx)
    return pl.pallas_call(
        _wa_body,
        grid=grid,
        in_specs=[pl.BlockSpec(memory_space=pltpu.SMEM), q_spec, *kv_specs],
        out_specs=q_spec,
        out_shape=jax.ShapeDtypeStruct((bsz * seq, dq), BF16),
        compiler_params=_cparams(2),
        name=name,
    )(sinks, q, k, v, k, v)


def _na_bias_table(rpb):
    qcol = np.arange(GRID_W)[:, None]
    kcol = np.arange(GRID_W)[None, :]
    win_start = np.clip(qcol - NA_KW // 2, 0, GRID_W - NA_KW)
    valid = (kcol >= win_start) & (kcol < win_start + NA_KW)
    off = np.clip(kcol - qcol, -(NA_KW - 1), NA_KW - 1) + NA_KW - 1
    n_off = 2 * NA_KW - 1
    onehot = (off[None] == np.arange(n_off)[:, None, None]).astype(np.float32)
    pair = np.zeros((2 * n_off, GRID_W, 2 * GRID_W), np.float32)
    pair[:n_off, :, :GRID_W] = onehot
    pair[n_off:, :, GRID_W:] = onehot
    rpb2 = jnp.concatenate([rpb[:, :-1], rpb[:, 1:]], axis=-1).astype(F32)
    t = jnp.einsum("hdo,oqk->hdqk", rpb2, pair, precision=lax.Precision.HIGHEST)
    return jnp.where(np.concatenate([valid, valid], axis=1), t * LOG2E, NEG_INF)


def _rope_tables(seq, pad_rows):
    t = np.arange(seq)
    n = HEAD_DIM // 4
    inv_freq = ROPE_BASE ** (-np.arange(n, dtype=np.float64) / n)
    ang_r = (t // GRID_W)[:, None] * inv_freq[None, :]
    ang_c = (t % GRID_W)[:, None] * inv_freq[None, :]
    ang = np.concatenate([ang_r, ang_r, ang_c, ang_c], axis=1)
    sign = np.tile(np.concatenate([-np.ones(n), np.ones(n)]), 2)
    cos = np.concatenate([np.cos(ang), np.ones((pad_rows, HEAD_DIM))], axis=0)
    sin = np.concatenate([np.sin(ang) * sign, np.zeros((pad_rows, HEAD_DIM))], axis=0)
    reps = (1, LANES // HEAD_DIM)
    return jnp.asarray(np.tile(cos, reps), F32), jnp.asarray(np.tile(sin, reps), F32)


def kernel(x, c, ctx, c_ctx, w_mod, b_mod, ln_g, ln_b, ffn_w_in, ffn_w_out,
           na_w_qkv, na_w_o, na_rpb, wa_w_qkv, wa_w_o, wa_sinks):
    bsz, seq, d = x.shape
    n_ctx = ctx.shape[1]
    depth = w_mod.shape[0]
    d_ff = ffn_w_out.shape[2]
    alpha = (2 * depth) ** 0.25
    t_lat, t_ctx = bsz * seq, bsz * n_ctx
    t_all = t_lat + t_ctx
    nt_lat, nt_all = t_lat // TOKEN_TILE, t_all // TOKEN_TILE
    tiles_per_seq = seq // TOKEN_TILE
    dq = N_HEADS * HEAD_DIM

    mod_rows = 16
    cc = jnp.concatenate([c, c_ctx[None], jnp.zeros((mod_rows - bsz - 1, d), F32)], axis=0)
    mods = _modulation(cc, w_mod, b_mod)

    w_in, w_out = ffn_w_in[0, 0].astype(BF16), ffn_w_out[0, 0].astype(BF16)
    ffn_w_specs = [_resident((d, 2 * d_ff)), _resident((d_ff, d))]
    h_parts = [x.reshape(t_lat, d), ctx.reshape(t_ctx, d)]
    ctx_blk0 = t_lat // n_ctx

    for i in range(depth):
        last = i == depth - 1
        is_na = i % 2 == 0
        j = i // 2
        mod_spec = _mod_spec(i, nt_lat, tiles_per_seq, bsz, d)
        ln_spec = pl.BlockSpec((None, 3, d), lambda t, i=i: (i, 0, 0))
        mixer_qkv, mixer_o = (na_w_qkv, na_w_o) if is_na else (wa_w_qkv, wa_w_o)

        h_specs = _split_tok_specs(d, nt_lat) if len(h_parts) == 2 else [_tok_spec(d)]
        jobs = [_cast_job(ffn_w_in, (i, 1), CAST_ROWS, nt_all), _cast_job(ffn_w_out, (i, 1), CAST_ROWS_OUT, nt_all),
                _cast_job(mixer_qkv, (j,), CAST_ROWS, nt_all), _cast_job(mixer_o, (j,), CAST_ROWS, nt_all)]
        c_in, c_out, c_shape, c_args = zip(*jobs)
        h, w_in, w_out, w_qkv, w_o = pl.pallas_call(
            functools.partial(_ffn_body, alpha=alpha, n_x=len(h_parts), n_lat_tiles=nt_lat, n_cast=len(jobs)),
            grid=(nt_all,),
            in_specs=[mod_spec, *h_specs, *ffn_w_specs, ln_spec, ln_spec, *c_in],
            out_specs=[_tok_spec(d), *c_out],
            out_shape=[jax.ShapeDtypeStruct((t_all, d), F32), *c_shape],
            compiler_params=_cparams(1),
            name=f"ffn_pre_{i}",
        )(mods, *h_parts, w_in, w_out, ln_g, ln_b, *c_args)

        if is_na:
            q, k, v = pl.pallas_call(
                _qkv_na_body,
                grid=(nt_all,),
                in_specs=[mod_spec, _tok_spec(d), _resident(w_qkv.shape)],
                out_specs=[_tok_spec(dq)] * 3,
                out_shape=[jax.ShapeDtypeStruct((t_all, dq), BF16)] * 3,
                compiler_params=_cparams(1),
                name=f"qkv_na_{i}",
            )(mods, h, w_qkv)
            att = _na_attention(q, k, v, _na_bias_table(na_rpb[j]), bsz, seq, n_ctx, f"attn_na_{i}")
        else:
            nkv = WA_KV_HEADS * LANES
            cos, sin = _rope_tables(seq, TOKEN_TILE)
            rope_spec = pl.BlockSpec(
                (TOKEN_TILE, LANES), lambda t: (jnp.where(t < nt_lat, t % tiles_per_seq, tiles_per_seq), 0))
            q, k, v = pl.pallas_call(
                _qkv_wa_body,
                grid=(nt_all,),
                in_specs=[mod_spec, _tok_spec(d), rope_spec, rope_spec, _resident(w_qkv.shape)],
                out_specs=[_tok_spec(dq), _tok_spec(nkv), _tok_spec(nkv)],
                out_shape=[jax.ShapeDtypeStruct((t_all, dq), BF16),
                           jax.ShapeDtypeStruct((t_all, nkv), BF16),
                           jax.ShapeDtypeStruct((t_all, nkv), BF16)],
                compiler_params=_cparams(1),
                name=f"qkv_wa_{i}",
            )(mods, h, cos, sin, w_qkv)
            att = _wa_attention(wa_sinks[j], q, k, v, bsz, seq, n_ctx, f"attn_wa_{i}")

        att_parts = [att]
        if not last:
            ctx_spec = pl.BlockSpec((n_ctx, dq), lambda b: (ctx_blk0 + b, 0))
            att_parts.append(pl.pallas_call(
                _ctx_body,
                grid=(bsz,),
                in_specs=[ctx_spec] * 3,
                out_specs=pl.BlockSpec((n_ctx, dq), lambda b: (b, 0)),
                out_shape=jax.ShapeDtypeStruct((t_ctx, dq), BF16),
                compiler_params=_cparams(1),
                name=f"attn_ctx_{i}",
            )(q, k, v))

        n_tiles = nt_lat if last else nt_all
        att_specs = _split_tok_specs(dq, nt_lat) if len(att_parts) == 2 else [_tok_spec(dq)]
        jobs = [] if last else [_cast_job(ffn_w_in, (i + 1, 0), CAST_ROWS, n_tiles),
                                _cast_job(ffn_w_out, (i + 1, 0), CAST_ROWS_OUT, n_tiles)]
        c_in, c_out, c_shape, c_args = zip(*jobs) if jobs else ((), (), (), ())
        h, *next_w = pl.pallas_call(
            functools.partial(_post_body, alpha=alpha, n_a=len(att_parts), n_lat_tiles=nt_lat, n_cast=len(jobs)),
            grid=(n_tiles,),
            in_specs=[mod_spec, _tok_spec(d), *att_specs, _resident(w_o.shape), *ffn_w_specs,
                      ln_spec, ln_spec, *c_in],
            out_specs=[_tok_spec(d), *c_out],
            out_shape=[jax.ShapeDtypeStruct((n_tiles * TOKEN_TILE, d), F32), *c_shape],
            compiler_params=_cparams(1),
            name=f"post_{i}",
        )(mods, h, *att_parts, w_o, w_in, w_out, ln_g, ln_b, *c_args)
        if next_w:
            w_in, w_out = next_w
        h_parts = [h]

    return h.reshape(bsz, seq, d)
```
